```python
import math
import jax, jax.numpy as jnp
from jax import lax
import numpy as np

D_MODEL = 1024
BATCH = 8
SEQ = 2048
DEPTH = 4
DEC_BATCH = 128
DEC_SEQ = 1
PAST_LEN = 2048
PAGE_SIZE = 128

N_AB_LAYERS = (DEPTH + 1) // 2
N_C_LAYERS = DEPTH // 2
SB_HEADS = 8
SB_HEAD_DIM = D_MODEL // 16
SB_WIDTH = SB_HEADS * SB_HEAD_DIM
SB_BLOCK = 128
RET_HEADS = 4
RET_KEY_DIM = D_MODEL // 8
RET_VAL_DIM = D_MODEL // 8
RET_K_WIDTH = RET_HEADS * RET_KEY_DIM
RET_V_WIDTH = RET_HEADS * RET_VAL_DIM
RET_CHUNK = 128
ROPE_BASE = 10000.0
AB_SPLITS = (SB_WIDTH, SB_WIDTH, SB_WIDTH, RET_K_WIDTH, RET_K_WIDTH, RET_V_WIDTH, RET_V_WIDTH)
AB_IN_WIDTH = sum(AB_SPLITS)
AB_MIX_WIDTH = SB_WIDTH + RET_V_WIDTH
SSD_WIDTH = 2 * D_MODEL
SSD_HEAD_DIM = 64
SSD_HEADS = SSD_WIDTH // SSD_HEAD_DIM
SSD_GROUPS = 4
SSD_STATE = 128
SSD_CONV = 4
SSD_CHUNK = 128
SSD_CONV_DIM = SSD_WIDTH + 2 * SSD_GROUPS * SSD_STATE
SSD_SPLITS = (SSD_WIDTH, SSD_CONV_DIM, SSD_HEADS)
SSD_IN_WIDTH = sum(SSD_SPLITS)
N_EXPERTS = 32
TOP_K = 4
D_EXPERT = D_MODEL
SWIGLU_LIMIT = 7.0
SWIGLU_ALPHA = 1.702
MOE_BLOCK = 128
EPS = 1e-6

kernel_name = 'hybrid_stickbreak_retention_ssd_moe_step'


def _split(a, sizes):
    return jnp.split(a, np.cumsum(sizes)[:-1].tolist(), axis=-1)


def rms_norm(x, w):
    xf = x.astype(jnp.float32)
    y = xf * lax.rsqrt(jnp.mean(xf * xf, axis=-1, keepdims=True) + EPS)
    return (y * w.astype(jnp.float32)).astype(x.dtype)


def ada_mod(c, w_ada, b_ada):
    mod = jax.nn.silu(c) @ w_ada + b_ada
    return jnp.split(mod[:, None, :], 6, axis=-1)


def modulate(x, w, shift, scale):
    return rms_norm(x, w) * (1 + scale) + shift


def rotary(x, pos):
    half = x.shape[-1] // 2
    inv = ROPE_BASE ** (-jnp.arange(half, dtype=jnp.float32) / half)
    ang = pos.astype(jnp.float32)[:, None] * inv[None, :]
    cos = jnp.cos(ang)[None, :, None, :]
    sin = jnp.sin(ang)[None, :, None, :]
    x = x.astype(jnp.float32)
    x1, x2 = x[..., :half], x[..., half:]
    return jnp.concatenate([x1 * cos - x2 * sin, x1 * sin + x2 * cos], axis=-1)


def stick_breaking_block(q, q_pos, k, v, k_pos, bias):
    z = jnp.einsum('bqhd,bshd->bhqs', q, k).astype(jnp.float32) * (SB_HEAD_DIM ** -0.5)
    z = z + bias.astype(jnp.float32)[None, :, None, None]
    causal = (k_pos[None, :] < q_pos[:, None])[None, None]
    log_keep = jnp.where(causal, jax.nn.log_sigmoid(-z), 0.0)
    between = lax.cumsum(log_keep, axis=3, reverse=True) - log_keep
    w = jnp.where(causal, jnp.exp(jax.nn.log_sigmoid(z) + between), 0.0)
    return jnp.einsum('bhqs,bshd->bqhd', w, v.astype(jnp.float32))


def stick_breaking(q, q_pos, k, v, k_pos, bias, block):
    b, L, H, D = q.shape
    nb = L // block
    qb = jnp.moveaxis(q.reshape(b, nb, block, H, D), 1, 0)
    pb = q_pos.reshape(nb, block)
    ob = lax.map(lambda a: stick_breaking_block(a[0], a[1], k, v, k_pos, bias), (qb, pb))
    return jnp.moveaxis(ob, 0, 1).reshape(b, L, H, D)


def retention_chunk(R, q, k, v, log_gamma):
    L = q.shape[1]
    idx = jnp.arange(L)
    diff = idx[:, None] - idx[None, :]
    causal = diff >= 0
    decay = jnp.where(causal[None], jnp.exp(jnp.where(causal, diff, 0)[None].astype(jnp.float32) * log_gamma[:, None, None]), 0.0)
    scores = jnp.einsum('blhk,bmhk->bhlm', q, k) * decay[None]
    inner = jnp.einsum('bhlm,bmhv->blhv', scores, v)
    from_state = jnp.einsum('blhk,bhkv->blhv', q, R) * jnp.exp((idx + 1)[:, None] * log_gamma[None, :])[None, :, :, None]
    k_dec = k * jnp.exp((L - 1 - idx)[:, None] * log_gamma[None, :])[None, :, :, None]
    R_new = jnp.exp(L * log_gamma)[None, :, None, None] * R + jnp.einsum('blhk,blhv->bhkv', k_dec, v)
    return R_new, inner + from_state


def retention(R0, q, k, v, log_gamma, chunk):
    b, L, H, _ = q.shape
    nc = L // chunk
    to_chunks = lambda a: jnp.moveaxis(a.reshape(b, nc, chunk, H, a.shape[-1]), 1, 0)
    R, o = lax.scan(lambda R, a: retention_chunk(R, a[0], a[1], a[2], log_gamma), R0,
                    (to_chunks(q), to_chunks(k), to_chunks(v)))
    return jnp.moveaxis(o, 0, 1).reshape(b, L, H, v.shape[-1]), R


def ab_mixer(h, pos, k_past, v_past, past_pos, R0, w_in, w_out, gn_w, sb_bias, sb_block, ret_chunk):
    b, L, _ = h.shape
    qa, ka, va, qr, kr, vr, gr = _split(h @ w_in, AB_SPLITS)
    sb = lambda a: a.reshape(b, L, SB_HEADS, SB_HEAD_DIM)
    rt = lambda a, d: a.reshape(b, L, RET_HEADS, d)
    qa, ka, va = sb(qa), sb(ka), sb(va)
    k_all = jnp.concatenate([k_past.astype(ka.dtype), ka], axis=1)
    v_all = jnp.concatenate([v_past.astype(va.dtype), va], axis=1)
    k_pos = jnp.concatenate([past_pos, pos])
    o_sb = stick_breaking(qa, pos, k_all, v_all, k_pos, sb_bias, sb_block)
    log_gamma = jnp.log1p(-jnp.exp2(-5.0 - jnp.arange(RET_HEADS, dtype=jnp.float32)))
    q_r = rotary(rt(qr, RET_KEY_DIM), pos)
    k_r = rotary(rt(kr, RET_KEY_DIM), pos) * (RET_KEY_DIM ** -0.5)
    o_r, R = retention(R0.astype(jnp.float32), q_r, k_r, rt(vr, RET_VAL_DIM).astype(jnp.float32), log_gamma, ret_chunk)
    mu = jnp.mean(o_r, axis=-1, keepdims=True)
    var = jnp.mean(jnp.square(o_r - mu), axis=-1, keepdims=True)
    o_r = ((o_r - mu) * lax.rsqrt(var + EPS)).reshape(b, L, RET_V_WIDTH) * gn_w.astype(jnp.float32)
    mixed = jnp.concatenate([o_sb.reshape(b, L, SB_WIDTH), jax.nn.silu(gr.astype(jnp.float32)) * o_r], axis=-1)
    return mixed.astype(h.dtype) @ w_out, ka, va, R


def ssd_chunk(S, x, dt, A, Bm, Cm):
    L = x.shape[1]
    acs = jnp.cumsum(dt * A, axis=1)
    causal = (jnp.arange(L)[:, None] >= jnp.arange(L)[None, :])[None, :, :, None, None]
    seg = jnp.where(causal, acs[:, :, None] - acs[:, None, :], -jnp.inf)
    cb = jnp.einsum('blgn,bsgn->blsg', Cm, Bm)
    w = cb[..., None] * jnp.exp(seg) * dt[:, None]
    y = jnp.einsum('blsgr,bsgrp->blgrp', w, x)
    y = y + jnp.einsum('blgn,bgrpn->blgrp', Cm, S) * jnp.exp(acs)[..., None]
    to_end = jnp.exp(acs[:, -1:] - acs) * dt
    S_new = jnp.exp(acs[:, -1])[..., None, None] * S + jnp.einsum('bsgn,bsgr,bsgrp->bgrpn', Bm, to_end, x)
    return S_new, y


def ssd_mixer(h, conv_prev, S0, w_in, conv_w, conv_b, dt_bias, a_log, d_skip, norm_w, w_out, chunk):
    b, L, _ = h.shape
    G, R, P, N = SSD_GROUPS, SSD_HEADS // SSD_GROUPS, SSD_HEAD_DIM, SSD_STATE
    z, xbc, dt = _split(h @ w_in, SSD_SPLITS)
    xin = jnp.concatenate([conv_prev.astype(xbc.dtype), xbc], axis=1)
    conv = conv_b + sum(xin[:, j:j + L] * conv_w[j] for j in range(SSD_CONV))
    xs, Bm, Cm = _split(jax.nn.silu(conv).astype(jnp.float32), (SSD_WIDTH, G * N, G * N))
    xs = xs.reshape(b, L, G, R, P)
    Bm = Bm.reshape(b, L, G, N)
    Cm = Cm.reshape(b, L, G, N)
    dt = jax.nn.softplus(dt.astype(jnp.float32) + dt_bias.astype(jnp.float32)).reshape(b, L, G, R)
    A = -jnp.exp(a_log.astype(jnp.float32)).reshape(G, R)
    nc = L // chunk
    to_chunks = lambda a: jnp.moveaxis(a.reshape(b, nc, chunk, *a.shape[2:]), 1, 0)
    S, y = lax.scan(lambda S, a: ssd_chunk(S, a[0], a[1], A, a[2], a[3]),
                    S0.astype(jnp.float32).reshape(b, G, R, P, N),
                    (to_chunks(xs), to_chunks(dt), to_chunks(Bm), to_chunks(Cm)))
    y = jnp.moveaxis(y, 0, 1).reshape(b, L, G, R, P) + d_skip.astype(jnp.float32).reshape(G, R, 1) * xs
    yg = y.reshape(b, L, G, SSD_WIDTH // G) * jax.nn.silu(z.astype(jnp.float32)).reshape(b, L, G, SSD_WIDTH // G)
    yg = yg * lax.rsqrt(jnp.mean(yg * yg, axis=-1, keepdims=True) + EPS)
    yg = yg.reshape(b, L, SSD_WIDTH) * norm_w.astype(jnp.float32)
    return yg.astype(h.dtype) @ w_out, xin[:, L:], S.reshape(b, SSD_HEADS, P, N)


def moe(x, router_w, router_b, w1, b1, w2, b2):
    T, D = x.shape
    logits = (x @ router_w).astype(jnp.float32) + router_b.astype(jnp.float32)
    top_val, top_idx = lax.top_k(logits, TOP_K)
    gate = jax.nn.softmax(top_val, axis=-1)
    n_assign = T * TOP_K
    flat_e = top_idx.reshape(n_assign)
    order = jnp.argsort(flat_e, stable=True)
    e_sorted = flat_e[order]
    counts = jnp.bincount(flat_e, length=N_EXPERTS)
    padded = (counts + MOE_BLOCK - 1) // MOE_BLOCK * MOE_BLOCK
    pad_end = jnp.cumsum(padded)
    rank = jnp.arange(n_assign) - (jnp.cumsum(counts) - counts)[e_sorted]
    dest = (pad_end - padded)[e_sorted] + rank
    n_blocks = -(-n_assign // MOE_BLOCK) + N_EXPERTS
    n_rows = n_blocks * MOE_BLOCK
    row_tok = jnp.full((n_rows,), T, jnp.int32).at[dest].set((order // TOP_K).astype(jnp.int32))
    row_gate = jnp.zeros((n_rows,), jnp.float32).at[dest].set(gate.reshape(n_assign)[order])
    block_expert = jnp.minimum(jnp.searchsorted(pad_end, jnp.arange(n_blocks) * MOE_BLOCK, side='right'), N_EXPERTS - 1)
    xb = jnp.concatenate([x, jnp.zeros((1, D), x.dtype)])[row_tok].reshape(n_blocks, MOE_BLOCK, D)

    def expert_block(a):
        xe, e = a
        hid = xe @ w1[e] + b1[e]
        glu = jnp.minimum(hid[:, :D_EXPERT], SWIGLU_LIMIT)
        lin = jnp.clip(hid[:, D_EXPERT:], -SWIGLU_LIMIT, SWIGLU_LIMIT)
        return (glu * jax.nn.sigmoid(SWIGLU_ALPHA * glu) * (lin + 1)) @ w2[e] + b2[e]

    yb = lax.map(expert_block, (xb, block_expert))
    y = jnp.zeros((T + 1, D), jnp.float32).at[row_tok].add(yb.reshape(n_rows, D).astype(jnp.float32) * row_gate[:, None])
    return y[:T].astype(x.dtype)


def setup_inputs(seed: int = 0) -> dict:
    key = jax.random.key(seed)
    ks = iter(jax.random.split(key, 48))

    def nrm(shape, scale):
        return jax.random.normal(next(ks), shape, jnp.float32) * scale

    def unif(shape, lo, hi):
        return jax.random.uniform(next(ks), shape, jnp.float32, lo, hi)

    n_pages = PAST_LEN // PAGE_SIZE
    n_pool = (DEC_BATCH * n_pages * 5) // 4
    page_table = jax.random.permutation(next(ks), n_pool)[:DEC_BATCH * n_pages].reshape(DEC_BATCH, n_pages).astype(jnp.int32)
    dt0 = jnp.exp(unif((N_C_LAYERS, SSD_HEADS), math.log(1e-3), math.log(1e-1)))
    return {
        'x_prompt': nrm((BATCH, SEQ, D_MODEL), 1.0),
        'x_sample': nrm((DEC_BATCH, DEC_SEQ, D_MODEL), 1.0),
        'cache_sb_k': nrm((N_AB_LAYERS, n_pool, PAGE_SIZE, SB_HEADS, SB_HEAD_DIM), 1.0),
        'cache_sb_v': nrm((N_AB_LAYERS, n_pool, PAGE_SIZE, SB_HEADS, SB_HEAD_DIM), 1.0),
        'page_table': page_table,
        'state_ret': nrm((N_AB_LAYERS, DEC_BATCH, RET_HEADS, RET_KEY_DIM, RET_VAL_DIM), 1.0),
        'state_ssd': nrm((N_C_LAYERS, DEC_BATCH, SSD_HEADS, SSD_HEAD_DIM, SSD_STATE), 0.1),
        'state_conv': nrm((N_C_LAYERS, DEC_BATCH, SSD_CONV - 1, SSD_CONV_DIM), 1.0),
        'c_prompt': nrm((BATCH, D_MODEL), 1.0),
        'c_sample': nrm((DEC_BATCH, D_MODEL), 1.0),
        'norm1_w': 1.0 + nrm((DEPTH, D_MODEL), 0.02),
        'norm2_w': 1.0 + nrm((DEPTH, D_MODEL), 0.02),
        'w_ada': nrm((DEPTH, D_MODEL, 6 * D_MODEL), 0.5 * D_MODEL ** -0.5),
        'b_ada': nrm((DEPTH, 6 * D_MODEL), 0.02),
        'ab_w_in': nrm((N_AB_LAYERS, D_MODEL, AB_IN_WIDTH), D_MODEL ** -0.5),
        'ab_w_out': nrm((N_AB_LAYERS, AB_MIX_WIDTH, D_MODEL), AB_MIX_WIDTH ** -0.5),
        'ret_gn_w': 1.0 + nrm((N_AB_LAYERS, RET_V_WIDTH), 0.02),
        'sb_bias': unif((N_AB_LAYERS, SB_HEADS), -8.0, -4.0),
        'ssd_w_in': nrm((N_C_LAYERS, D_MODEL, SSD_IN_WIDTH), D_MODEL ** -0.5),
        'ssd_conv_w': nrm((N_C_LAYERS, SSD_CONV, SSD_CONV_DIM), SSD_CONV ** -0.5),
        'ssd_conv_b': nrm((N_C_LAYERS, SSD_CONV_DIM), 0.02),
        'ssd_dt_bias': dt0 + jnp.log(-jnp.expm1(-dt0)),
        'ssd_a_log': jnp.log(unif((N_C_LAYERS, SSD_HEADS), 1.0, 16.0)),
        'ssd_d': 1.0 + nrm((N_C_LAYERS, SSD_HEADS), 0.1),
        'ssd_norm_w': 1.0 + nrm((N_C_LAYERS, SSD_WIDTH), 0.02),
        'ssd_w_out': nrm((N_C_LAYERS, SSD_WIDTH, D_MODEL), SSD_WIDTH ** -0.5),
        'router_w': nrm((DEPTH, D_MODEL, N_EXPERTS), D_MODEL ** -0.5),
        'router_b': nrm((DEPTH, N_EXPERTS), 0.01),
        'moe_w1': nrm((DEPTH, N_EXPERTS, D_MODEL, 2 * D_EXPERT), D_MODEL ** -0.5),
        'moe_b1': nrm((DEPTH, N_EXPERTS, 2 * D_EXPERT), 0.02),
        'moe_w2': nrm((DEPTH, N_EXPERTS, D_EXPERT, D_MODEL), D_EXPERT ** -0.5),
        'moe_b2': nrm((DEPTH, N_EXPERTS, D_MODEL), 0.02),
        'final_norm_w': 1.0 + nrm((D_MODEL,), 0.02),
    }


def reference(x_prompt, x_sample, cache_sb_k, cache_sb_v, page_table, state_ret, state_ssd, state_conv,
              c_prompt, c_sample, norm1_w, norm2_w, w_ada, b_ada, ab_w_in, ab_w_out, ret_gn_w, sb_bias,
              ssd_w_in, ssd_conv_w, ssd_conv_b, ssd_dt_bias, ssd_a_log, ssd_d, ssd_norm_w, ssd_w_out,
              router_w, router_b, moe_w1, moe_b1, moe_w2, moe_b2, final_norm_w):
    bp, sp, _ = x_prompt.shape
    bs, ls, _ = x_sample.shape
    past_len = page_table.shape[1] * cache_sb_k.shape[2]
    pos_p = jnp.arange(sp, dtype=jnp.int32)
    pos_s = past_len + jnp.arange(ls, dtype=jnp.int32)
    past_pos = jnp.arange(past_len, dtype=jnp.int32)
    no_pos = jnp.zeros((0,), jnp.int32)
    empty_kv = jnp.zeros((bp, 0, SB_HEADS, SB_HEAD_DIM), x_prompt.dtype)
    xp, xs = x_prompt, x_sample
    sbk_p, sbv_p, sbk_s, sbv_s, ret_p, ret_s = [], [], [], [], [], []
    ssd_p, ssd_s, conv_p, conv_s = [], [], [], []
    for layer in range(DEPTH):
        i = layer // 2
        sh1p, sc1p, g1p, sh2p, sc2p, g2p = ada_mod(c_prompt, w_ada[layer], b_ada[layer])
        sh1s, sc1s, g1s, sh2s, sc2s, g2s = ada_mod(c_sample, w_ada[layer], b_ada[layer])
        hp = modulate(xp, norm1_w[layer], sh1p, sc1p)
        hs = modulate(xs, norm1_w[layer], sh1s, sc1s)
        if layer % 2 == 0:
            o_p, k_p, v_p, R_p = ab_mixer(hp, pos_p, empty_kv, empty_kv, no_pos,
                                          jnp.zeros((bp, RET_HEADS, RET_KEY_DIM, RET_VAL_DIM), jnp.float32),
                                          ab_w_in[i], ab_w_out[i], ret_gn_w[i], sb_bias[i], SB_BLOCK, RET_CHUNK)
            k_past = cache_sb_k[i][page_table].reshape(bs, past_len, SB_HEADS, SB_HEAD_DIM)
            v_past = cache_sb_v[i][page_table].reshape(bs, past_len, SB_HEADS, SB_HEAD_DIM)
            o_s, k_s, v_s, R_s = ab_mixer(hs, pos_s, k_past, v_past, past_pos, state_ret[i],
                                          ab_w_in[i], ab_w_out[i], ret_gn_w[i], sb_bias[i], ls, ls)
            sbk_p.append(k_p)
            sbv_p.append(v_p)
            sbk_s.append(k_s)
            sbv_s.append(v_s)
            ret_p.append(R_p)
            ret_s.append(R_s)
        else:
            o_p, cv_p, S_p = ssd_mixer(hp, jnp.zeros((bp, SSD_CONV - 1, SSD_CONV_DIM), hp.dtype),
                                       jnp.zeros((bp, SSD_HEADS, SSD_HEAD_DIM, SSD_STATE), jnp.float32),
                                       ssd_w_in[i], ssd_conv_w[i], ssd_conv_b[i], ssd_dt_bias[i], ssd_a_log[i],
                                       ssd_d[i], ssd_norm_w[i], ssd_w_out[i], SSD_CHUNK)
            o_s, cv_s, S_s = ssd_mixer(hs, state_conv[i], state_ssd[i],
                                       ssd_w_in[i], ssd_conv_w[i], ssd_conv_b[i], ssd_dt_bias[i], ssd_a_log[i],
                                       ssd_d[i], ssd_norm_w[i], ssd_w_out[i], ls)
            ssd_p.append(S_p)
            ssd_s.append(S_s)
            conv_p.append(cv_p)
            conv_s.append(cv_s)
        xp = xp + g1p * o_p
        xs = xs + g1s * o_s
        hp = modulate(xp, norm2_w[layer], sh2p, sc2p)
        hs = modulate(xs, norm2_w[layer], sh2s, sc2s)
        ym = moe(jnp.concatenate([hp.reshape(bp * sp, D_MODEL), hs.reshape(bs * ls, D_MODEL)], axis=0),
                 router_w[layer], router_b[layer], moe_w1[layer], moe_b1[layer], moe_w2[layer], moe_b2[layer])
        xp = xp + g2p * ym[:bp * sp].reshape(bp, sp, D_MODEL)
        xs = xs + g2s * ym[bp * sp:].reshape(bs, ls, D_MODEL)
    y_prompt = rms_norm(xp, final_norm_w)
    y_sample = rms_norm(xs, final_norm_w)
    st = lambda lst: jnp.stack(lst, axis=0).astype(x_prompt.dtype)
    return (y_prompt, y_sample, st(sbk_p), st(sbv_p), st(sbk_s), st(sbv_s), st(ret_p), st(ret_s),
            st(ssd_p), st(ssd_s), st(conv_p), st(conv_s))
```

```python
import functools
import math

import jax
import jax.numpy as jnp
from jax import lax
from jax.experimental import pallas as pl
from jax.experimental.pallas import tpu as pltpu

F32 = jnp.float32
BF16 = jnp.bfloat16

EPS = 1e-6
ROPE_BASE = 10000.0
SB_HEADS = 8
SB_HEAD_DIM = 64
SB_WIDTH = SB_HEADS * SB_HEAD_DIM
RET_HEADS = 4
RET_DIM = 128
RET_WIDTH = RET_HEADS * RET_DIM
SSD_HEAD_DIM = 64
SSD_GROUPS = 4
SSD_STATE = 128
SSD_CONV = 4
TOP_K = 4
SWIGLU_LIMIT = 7.0
SWIGLU_ALPHA = 1.702
CHUNK = 128
LANES = 128
ROW_TILE = 256
MOE_ROWS = 256
VMEM_LIMIT = 56 * 1024 * 1024


def _cparams(sem):
    return pltpu.CompilerParams(dimension_semantics=sem, vmem_limit_bytes=VMEM_LIMIT)


def _silu(x):
    return x * jax.nn.sigmoid(x)


def _softplus(x):
    return jnp.maximum(x, 0.0) + jnp.log1p(jnp.exp(-jnp.abs(x)))


def _modulate(x, nw, shift, scale):
    y = x * lax.rsqrt(jnp.mean(x * x, axis=-1, keepdims=True) + EPS)
    return (y * nw) * (1.0 + scale) + shift


def _split_bf16(x):
    hi = x.astype(BF16)
    lo = (x - hi.astype(F32)).astype(BF16)
    return hi, lo


def _dot(a, b):
    return jnp.dot(a, b, preferred_element_type=F32)


def _dot_nt(a, b):
    return lax.dot_general(a, b, (((1,), (1,)), ((), ())), preferred_element_type=F32)


def _dot_tn(a, b):
    return lax.dot_general(a, b, (((0,), (0,)), ((), ())), preferred_element_type=F32)


def _tri(n, strict, lower):
    r = lax.broadcasted_iota(jnp.int32, (n, n), 0)
    c = lax.broadcasted_iota(jnp.int32, (n, n), 1)
    if lower:
        m = (c < r) if strict else (c <= r)
    else:
        m = (c > r) if strict else (c >= r)
    return m


def _mod_specs(kind, ks, tm, tiles_per_seq, d):
    if kind == "p":
        return [pl.BlockSpec((None, 1, d), lambda i, k=k: ((i // tiles_per_seq) * 6 + k, 0, 0)) for k in ks]
    return [pl.BlockSpec((tm, d), lambda i, k=k: (i, k)) for k in ks]


def _ada_kernel(c_ref, w_ref, b_ref, o_ref):
    c = _silu(c_ref[...]).astype(BF16)
    o_ref[...] = _dot(c, w_ref[...].astype(BF16)) + b_ref[...]


def _ada_mod(c, w_ada, b_ada):
    depth, d, n = w_ada.shape
    rows = c.shape[0]
    tn = 1024
    return pl.pallas_call(
        _ada_kernel,
        grid=(depth, n // tn),
        in_specs=[pl.BlockSpec((rows, d), lambda l, j: (0, 0)),
                  pl.BlockSpec((None, d, tn), lambda l, j: (l, 0, j)),
                  pl.BlockSpec((None, 1, tn), lambda l, j: (l, 0, j))],
        out_specs=pl.BlockSpec((None, rows, tn), lambda l, j: (l, 0, j)),
        out_shape=jax.ShapeDtypeStruct((depth, rows, n), F32),
        compiler_params=_cparams(("arbitrary", "arbitrary")),
        name="ada_mod",
    )(c, w_ada, b_ada.reshape(depth, 1, n))


def _inproj_kernel(x_ref, nw_ref, sh_ref, sc_ref, *refs, plan):
    n_w = len(plan)
    w_refs, o_refs = refs[:n_w], refs[n_w:]
    h = _modulate(x_ref[...], nw_ref[...], sh_ref[...], sc_ref[...]).astype(BF16)
    oi = 0
    for w_ref, dts in zip(w_refs, plan):
        y = _dot(h, w_ref[...])
        for dt in dts:
            o_refs[oi][...] = y.astype(dt)
            oi += 1


def _inproj(x, nw, mod, kind, ws, plan, tiles_per_seq, name):
    t, d = x.shape
    tm = min(ROW_TILE, t)
    out_shape, out_specs = [], []
    for w, dts in zip(ws, plan):
        for dt in dts:
            out_shape.append(jax.ShapeDtypeStruct((t, w.shape[1]), dt))
            out_specs.append(pl.BlockSpec((tm, w.shape[1]), lambda i: (i, 0)))
    return pl.pallas_call(
        functools.partial(_inproj_kernel, plan=plan),
        grid=(t // tm,),
        in_specs=[pl.BlockSpec((tm, d), lambda i: (i, 0)),
                  pl.BlockSpec((1, d), lambda i: (0, 0))]
        + _mod_specs(kind, (0, 1), tm, tiles_per_seq, d)
        + [pl.BlockSpec(w.shape, lambda i: (0, 0)) for w in ws],
        out_specs=out_specs,
        out_shape=out_shape,
        compiler_params=_cparams(("arbitrary",)),
        name=name,
    )(x, nw, mod, mod, *ws)


def _post_kernel(m_ref, w_ref, x_ref, g_ref, nw_ref, sh_ref, sc_ref, rwh_ref, rwl_ref, rb_ref,
                 xo_ref, h_ref, lg_ref):
    o = _dot(m_ref[...], w_ref[...])
    xn = x_ref[...] + g_ref[...] * o
    xo_ref[...] = xn
    h = _modulate(xn, nw_ref[...], sh_ref[...], sc_ref[...])
    h_ref[...] = h.astype(h_ref.dtype)
    hh, hl = _split_bf16(h)
    lg_ref[...] = _dot(hh, rwh_ref[...]) + _dot(hl, rwh_ref[...]) + _dot(hh, rwl_ref[...]) + rb_ref[...]


def _post(mixed, w_out, x, nw2, mod, kind, rwh, rwl, rb, tiles_per_seq, name):
    t, d = x.shape
    kd = mixed.shape[1]
    tm = min(ROW_TILE, t)
    row = lambda n: pl.BlockSpec((tm, n), lambda i: (i, 0))
    full = lambda a: pl.BlockSpec(a.shape, lambda i: (0, 0))
    g_spec, sh_spec, sc_spec = _mod_specs(kind, (2, 3, 4), tm, tiles_per_seq, d)
    return pl.pallas_call(
        _post_kernel,
        grid=(t // tm,),
        in_specs=[row(kd), full(w_out), row(d), g_spec, full(nw2), sh_spec, sc_spec,
                  full(rwh), full(rwl), full(rb)],
        out_specs=[row(d), row(d), row(LANES)],
        out_shape=[jax.ShapeDtypeStruct((t, d), F32), jax.ShapeDtypeStruct((t, d), BF16),
                   jax.ShapeDtypeStruct((t, LANES), F32)],
        compiler_params=_cparams(("arbitrary",)),
        name=name,
    )(mixed, w_out, x, mod, nw2, mod, mod, rwh, rwl, rb)


def _sb_block(qh, kh, vh, bias, run, upper, causal):
    z = _dot_nt(qh, kh) * (SB_HEAD_DIM ** -0.5) + bias
    sp = _softplus(z)
    lk = -sp
    if causal is not None:
        lk = jnp.where(causal, lk, 0.0)
    hi, lo = _split_bf16(lk)
    between = _dot(hi, upper) + _dot(lo, upper) + run
    w = jnp.exp((z - sp) + between)
    if causal is not None:
        w = jnp.where(causal, w, 0.0)
    return _dot(w.astype(BF16), vh), run + jnp.sum(lk, axis=-1, keepdims=True)


def _sb_prompt_kernel(bias_ref, q_ref, k_ref, v_ref, o_ref):
    qi = pl.program_id(1)
    upper = _tri(CHUNK, strict=True, lower=True).astype(BF16)
    causal = _tri(CHUNK, strict=True, lower=True)
    for h in range(SB_HEADS):
        cols = slice(h * SB_HEAD_DIM, (h + 1) * SB_HEAD_DIM)
        qh = q_ref[:, cols]
        bias = bias_ref[h]
        d0 = pl.multiple_of(qi * CHUNK, CHUNK)
        acc, run = _sb_block(qh, k_ref[pl.ds(d0, CHUNK), cols], v_ref[pl.ds(d0, CHUNK), cols], bias,
                             jnp.zeros((CHUNK, 1), F32), upper, causal)

        def body(t, carry, qh=qh, cols=cols, bias=bias):
            acc, run = carry
            s0 = pl.multiple_of((qi - t) * CHUNK, CHUNK)
            a, run = _sb_block(qh, k_ref[pl.ds(s0, CHUNK), cols], v_ref[pl.ds(s0, CHUNK), cols], bias,
                               run, upper, None)
            return acc + a, run

        acc, _ = lax.fori_loop(1, qi + 1, body, (acc, run))
        o_ref[:, cols] = acc.astype(o_ref.dtype)


def _sb_prompt(q, k, v, bias, batch, seq):
    nq = seq // CHUNK
    return pl.pallas_call(
        _sb_prompt_kernel,
        grid=(batch, nq),
        in_specs=[pl.BlockSpec(memory_space=pltpu.SMEM),
                  pl.BlockSpec((CHUNK, SB_WIDTH), lambda b, i: (b * nq + i, 0)),
                  pl.BlockSpec((seq, SB_WIDTH), lambda b, i: (b, 0)),
                  pl.BlockSpec((seq, SB_WIDTH), lambda b, i: (b, 0))],
        out_specs=pl.BlockSpec((CHUNK, SB_WIDTH), lambda b, i: (b * nq + i, 0)),
        out_shape=jax.ShapeDtypeStruct((batch * seq, SB_WIDTH), BF16),
        compiler_params=_cparams(("arbitrary", "arbitrary")),
        name="sb_prompt",
    )(bias, q, k, v)


def _sb_sample_kernel(pt_ref, q_ref, bias_ref, *refs, n_pages):
    k_refs, v_refs, o_ref = refs[:n_pages], refs[n_pages:2 * n_pages], refs[2 * n_pages]
    row = lax.broadcasted_iota(jnp.int32, (SB_HEADS, SB_WIDTH), 0)
    col = lax.broadcasted_iota(jnp.int32, (SB_HEADS, SB_WIDTH), 1)
    own = (col // SB_HEAD_DIM) == row
    qm = jnp.where(own, jnp.broadcast_to(q_ref[...].astype(F32), (SB_HEADS, SB_WIDTH)), 0.0).astype(BF16)
    upper = _tri(CHUNK, strict=True, lower=True).astype(BF16)
    bias = bias_ref[...]
    acc = jnp.zeros((SB_HEADS, SB_WIDTH), F32)
    run = jnp.zeros((SB_HEADS, 1), F32)
    for p in reversed(range(n_pages)):
        kp = k_refs[p][...].astype(BF16)
        z = _dot_nt(qm, kp) * (SB_HEAD_DIM ** -0.5) + bias
        sp = _softplus(z)
        lk = -sp
        hi, lo = _split_bf16(lk)
        between = _dot(hi, upper) + _dot(lo, upper) + run
        w = jnp.exp((z - sp) + between)
        acc = acc + _dot(w.astype(BF16), v_refs[p][...].astype(BF16))
        run = run + jnp.sum(lk, axis=-1, keepdims=True)
    o_ref[...] = jnp.sum(jnp.where(own, acc, 0.0), axis=0, keepdims=True).astype(o_ref.dtype)


def _sb_sample(q, cache_k, cache_v, layer, page_table, bias):
    bs, n_pages = page_table.shape
    page = cache_k.shape[2]
    ck = cache_k.reshape(cache_k.shape[0], cache_k.shape[1], page, SB_WIDTH)
    cv = cache_v.reshape(cache_v.shape[0], cache_v.shape[1], page, SB_WIDTH)
    page_spec = lambda p: pl.BlockSpec((None, None, page, SB_WIDTH),
                                       lambda b, pt, p=p: (layer, pt[b * n_pages + p], 0, 0))
    grid_spec = pltpu.PrefetchScalarGridSpec(
        num_scalar_prefetch=1,
        grid=(bs,),
        in_specs=[pl.BlockSpec((None, 1, SB_WIDTH), lambda b, pt: (b, 0, 0)),
                  pl.BlockSpec((SB_HEADS, 1), lambda b, pt: (0, 0))]
        + [page_spec(p) for p in range(n_pages)] * 2,
        out_specs=pl.BlockSpec((None, 1, SB_WIDTH), lambda b, pt: (b, 0, 0)),
    )
    out = pl.pallas_call(
        functools.partial(_sb_sample_kernel, n_pages=n_pages),
        grid_spec=grid_spec,
        out_shape=jax.ShapeDtypeStruct((bs, 1, SB_WIDTH), BF16),
        compiler_params=_cparams(("arbitrary",)),
        name="sb_sample",
    )(page_table.reshape(-1), q.reshape(bs, 1, SB_WIDTH), bias.reshape(SB_HEADS, 1),
      *([ck] * n_pages), *([cv] * n_pages))
    return out.reshape(bs, SB_WIDTH)


def _log_gamma(h):
    return math.log1p(-(2.0 ** (-5.0 - h)))


def _rope_tables(pos):
    half = RET_DIM // 2
    inv = ROPE_BASE ** (-jnp.arange(half, dtype=F32) / half)
    ang = pos.astype(F32)[:, None] * inv[None, :]
    cos, sin = jnp.cos(ang), jnp.sin(ang)
    return jnp.concatenate([cos, cos], axis=-1), jnp.concatenate([-sin, sin], axis=-1)


def _rope(x, cos2, sin2):
    return x * cos2 + pltpu.roll(x, RET_DIM // 2, 1) * sin2


def _group_norm_gate(o, gn, g):
    mu = jnp.mean(o, axis=-1, keepdims=True)
    c = o - mu
    var = jnp.mean(c * c, axis=-1, keepdims=True)
    return _silu(g) * ((c * lax.rsqrt(var + EPS)) * gn)


def _ret_prompt_kernel(q_ref, k_ref, v_ref, g_ref, cos_ref, sin_ref, gn_ref, o_ref, r_ref, state):
    c = pl.program_id(1)

    @pl.when(c == 0)
    def _():
        state[...] = jnp.zeros_like(state)

    cos2, sin2 = cos_ref[...], sin_ref[...]
    li = lax.broadcasted_iota(jnp.int32, (CHUNK, CHUNK), 0)
    mi = lax.broadcasted_iota(jnp.int32, (CHUNK, CHUNK), 1)
    diff = li - mi
    idx = lax.broadcasted_iota(jnp.int32, (CHUNK, 1), 0).astype(F32)
    for h in range(RET_HEADS):
        lg = _log_gamma(h)
        cols = slice(h * RET_DIM, (h + 1) * RET_DIM)
        q = _rope(q_ref[:, cols], cos2, sin2)
        k = _rope(k_ref[:, cols], cos2, sin2) * (RET_DIM ** -0.5)
        v = v_ref[:, cols].astype(BF16)
        qb = q.astype(BF16)
        decay = jnp.where(diff >= 0, jnp.exp(jnp.maximum(diff, 0).astype(F32) * lg), 0.0)
        scores = _dot_nt(qb, k.astype(BF16)) * decay
        inner = _dot(scores.astype(BF16), v)
        r_old = state[h]
        from_state = _dot(qb, r_old.astype(BF16)) * jnp.exp((idx + 1.0) * lg)
        k_dec = k * jnp.exp((CHUNK - 1.0 - idx) * lg)
        state[h] = math.exp(CHUNK * lg) * r_old + _dot_tn(k_dec.astype(BF16), v)
        o = _group_norm_gate(inner + from_state, gn_ref[:, cols], g_ref[:, cols])
        o_ref[:, cols] = o.astype(o_ref.dtype)

    @pl.when(c == pl.num_programs(1) - 1)
    def _():
        r_ref[...] = state[...]


def _ret_prompt(ret, gn_w, batch, seq):
    nc = seq // CHUNK
    cos2, sin2 = _rope_tables(jnp.arange(seq, dtype=jnp.int32))
    col = lambda j: pl.BlockSpec((CHUNK, RET_WIDTH), lambda b, c, j=j: (b * nc + c, j))
    tab = pl.BlockSpec((CHUNK, RET_DIM), lambda b, c: (c, 0))
    return pl.pallas_call(
        _ret_prompt_kernel,
        grid=(batch, nc),
        in_specs=[col(0), col(1), col(2), col(3), tab, tab, pl.BlockSpec((1, RET_WIDTH), lambda b, c: (0, 0))],
        out_specs=[pl.BlockSpec((CHUNK, RET_WIDTH), lambda b, c: (b * nc + c, 0)),
                   pl.BlockSpec((None, RET_HEADS, RET_DIM, RET_DIM), lambda b, c: (b, 0, 0, 0))],
        out_shape=[jax.ShapeDtypeStruct((batch * seq, RET_WIDTH), BF16),
                   jax.ShapeDtypeStruct((batch, RET_HEADS, RET_DIM, RET_DIM), F32)],
        scratch_shapes=[pltpu.VMEM((RET_HEADS, RET_DIM, RET_DIM), F32)],
        compiler_params=_cparams(("arbitrary", "arbitrary")),
        name="ret_prompt",
    )(ret, ret, ret, ret, cos2, sin2, gn_w)


RET_SEQS = 8


def _ret_sample_kernel(q_ref, k_ref, v_ref, g_ref, cos_ref, sin_ref, gn_ref, r0_ref, o_ref, r_ref):
    cos2, sin2 = cos_ref[...], sin_ref[...]
    for h in range(RET_HEADS):
        gamma = math.exp(_log_gamma(h))
        cols = slice(h * RET_DIM, (h + 1) * RET_DIM)
        q = _rope(q_ref[:, cols], cos2, sin2)
        k = _rope(k_ref[:, cols], cos2, sin2) * (RET_DIM ** -0.5)
        v = v_ref[:, cols]
        qb, kb, vb = q.astype(BF16), k.astype(BF16), v.astype(BF16)
        qk = jnp.sum(qb.astype(F32) * kb.astype(F32), axis=-1, keepdims=True)
        inner = qk * vb.astype(F32)
        rows = []
        for s in range(RET_SEQS):
            r_old = r0_ref[s, h]
            rows.append(_dot(qb[s:s + 1, :], r_old.astype(BF16)) * gamma)
            k_col = jnp.transpose(kb[s:s + 1, :].astype(F32))
            r_ref[s, h] = gamma * r_old + k_col * vb[s:s + 1, :].astype(F32)
        o = inner + jnp.concatenate(rows, axis=0)
        o_ref[:, cols] = _group_norm_gate(o, gn_ref[:, cols], g_ref[:, cols]).astype(o_ref.dtype)


def _ret_sample(ret, gn_w, r0, past_len):
    bs = ret.shape[0]
    cos2, sin2 = _rope_tables(jnp.full((1,), past_len, jnp.int32))
    col = lambda j: pl.BlockSpec((RET_SEQS, RET_WIDTH), lambda i, j=j: (i, j))
    one = lambda n: pl.BlockSpec((1, n), lambda i: (0, 0))
    st = pl.BlockSpec((RET_SEQS, RET_HEADS, RET_DIM, RET_DIM), lambda i: (i, 0, 0, 0))
    return pl.pallas_call(
        _ret_sample_kernel,
        grid=(bs // RET_SEQS,),
        in_specs=[col(0), col(1), col(2), col(3), one(RET_DIM), one(RET_DIM), one(RET_WIDTH), st],
        out_specs=[pl.BlockSpec((RET_SEQS, RET_WIDTH), lambda i: (i, 0)), st],
        out_shape=[jax.ShapeDtypeStruct((bs, RET_WIDTH), BF16),
                   jax.ShapeDtypeStruct(r0.shape, F32)],
        compiler_params=_cparams(("arbitrary",)),
        name="ret_sample",
    )(ret, ret, ret, ret, cos2, sin2, gn_w, r0)


def _gated_group_rms(y, z, nw, groups):
    yg = y * _silu(z)
    gw = y.shape[-1] // groups
    outs = []
    for g in range(groups):
        a = yg[:, g * gw:(g + 1) * gw]
        outs.append(a * lax.rsqrt(jnp.mean(a * a, axis=-1, keepdims=True) + EPS))
    return jnp.concatenate(outs, axis=-1) * nw


def _ssd_prompt_kernel(z_ref, xbc_ref, dt_ref, cw_ref, cb_ref, dtb_ref, alog_ref, dsk_ref, nw_ref,
                       y_ref, s_ref, cv_ref, state, win, ybuf, *, heads, width):
    c = pl.program_id(1)
    hpg = heads // SSD_GROUPS
    gn = SSD_GROUPS * SSD_STATE

    @pl.when(c == 0)
    def _():
        state[...] = jnp.zeros_like(state)
        win[0:8, :] = jnp.zeros((8, win.shape[1]), F32)

    win[8:8 + CHUNK, :] = xbc_ref[...]
    conv = cb_ref[...]
    for j in range(SSD_CONV):
        conv = conv + win[5 + j:5 + j + CHUNK, :] * cw_ref[j:j + 1, :]
    act = _silu(conv)

    @pl.when(c == pl.num_programs(1) - 1)
    def _():
        cv_ref[...] = win[CHUNK + 5:CHUNK + 8, :]

    win[0:8, :] = win[CHUNK:CHUNK + 8, :]

    dt = _softplus(dt_ref[...] + dtb_ref[...])
    a = -jnp.exp(alog_ref[...])
    da_hi, da_lo = _split_bf16(dt * a)
    lower = _tri(CHUNK, strict=False, lower=True).astype(BF16)
    acs = _dot(lower, da_hi) + _dot(lower, da_lo)
    acs_t = jnp.transpose(acs)
    dt_t = jnp.transpose(dt)
    causal = _tri(CHUNK, strict=False, lower=True)
    for g in range(SSD_GROUPS):
        bg = act[:, width + g * SSD_STATE:width + (g + 1) * SSD_STATE].astype(BF16)
        cg = act[:, width + gn + g * SSD_STATE:width + gn + (g + 1) * SSD_STATE].astype(BF16)
        cbm = _dot_nt(cg, bg)
        for r in range(hpg):
            h = g * hpg + r
            cols = slice(h * SSD_HEAD_DIM, (h + 1) * SSD_HEAD_DIM)
            xh = act[:, cols]
            acs_col = acs[:, h:h + 1]
            seg = jnp.where(causal, acs_col - acs_t[h:h + 1, :], -jnp.inf)
            w = cbm * jnp.exp(seg) * dt_t[h:h + 1, :]
            s_old = state[h]
            y = _dot(w.astype(BF16), xh.astype(BF16))
            y = y + _dot_nt(cg, s_old.astype(BF16)) * jnp.exp(acs_col)
            last = acs[CHUNK - 1:CHUNK, h:h + 1]
            to_end = jnp.exp(last - acs_col) * dt[:, h:h + 1]
            state[h] = jnp.exp(last) * s_old + _dot_tn((xh * to_end).astype(BF16), bg)
            ybuf[:, cols] = y + dsk_ref[:, cols] * xh

    y_ref[...] = _gated_group_rms(ybuf[...], z_ref[...], nw_ref[...], SSD_GROUPS).astype(y_ref.dtype)

    @pl.when(c == pl.num_programs(1) - 1)
    def _():
        s_ref[...] = state[...]


def _ssd_prompt(z, xbc, dt, cw, cb, dtb, alog, dsk, nw, batch, seq, heads):
    nc = seq // CHUNK
    width = z.shape[1]
    cdim = xbc.shape[1]
    row = lambda n: pl.BlockSpec((CHUNK, n), lambda b, c: (b * nc + c, 0))
    full = lambda a: pl.BlockSpec(a.shape, lambda b, c: (0, 0))
    return pl.pallas_call(
        functools.partial(_ssd_prompt_kernel, heads=heads, width=width),
        grid=(batch, nc),
        in_specs=[row(width), row(cdim), row(LANES), full(cw), full(cb), full(dtb), full(alog), full(dsk), full(nw)],
        out_specs=[row(width),
                   pl.BlockSpec((None, heads, SSD_HEAD_DIM, SSD_STATE), lambda b, c: (b, 0, 0, 0)),
                   pl.BlockSpec((None, SSD_CONV - 1, cdim), lambda b, c: (b, 0, 0))],
        out_shape=[jax.ShapeDtypeStruct((batch * seq, width), BF16),
                   jax.ShapeDtypeStruct((batch, heads, SSD_HEAD_DIM, SSD_STATE), F32),
                   jax.ShapeDtypeStruct((batch, SSD_CONV - 1, cdim), F32)],
        scratch_shapes=[pltpu.VMEM((heads, SSD_HEAD_DIM, SSD_STATE), F32),
                        pltpu.VMEM((CHUNK + 8, cdim), F32),
                        pltpu.VMEM((CHUNK, width), F32)],
        compiler_params=_cparams(("arbitrary", "arbitrary")),
        name="ssd_prompt",
    )(z, xbc, dt, cw, cb, dtb, alog, dsk, nw)


SSD_SEQS = 4


def _ssd_sample_kernel(z_ref, xbc_ref, dt_ref, cprev_ref, cw_ref, cb_ref, dtb_ref, alog_ref, dsk_ref, nw_ref,
                       s0_ref, y_ref, s_ref, cv_ref, *, heads, width):
    hpg = heads // SSD_GROUPS
    gn = SSD_GROUPS * SSD_STATE
    for s in range(SSD_SEQS):
        dt = _softplus(dt_ref[s] + dtb_ref[...])
        decay = jnp.exp(dt * (-jnp.exp(alog_ref[...])))
        xrow = xbc_ref[s]
        prev = cprev_ref[s]
        conv = cb_ref[...] + xrow * cw_ref[SSD_CONV - 1:SSD_CONV, :]
        for j in range(SSD_CONV - 1):
            conv = conv + prev[j:j + 1, :] * cw_ref[j:j + 1, :]
        cv_ref[s, 0:SSD_CONV - 2, :] = prev[1:SSD_CONV - 1, :]
        cv_ref[s, SSD_CONV - 2:SSD_CONV - 1, :] = xrow
        act = _silu(conv)
        yrow = []
        for g in range(SSD_GROUPS):
            bg = act[:, width + g * SSD_STATE:width + (g + 1) * SSD_STATE].astype(BF16).astype(F32)
            cg = act[:, width + gn + g * SSD_STATE:width + gn + (g + 1) * SSD_STATE].astype(BF16)
            for r in range(hpg):
                h = g * hpg + r
                cols = slice(h * SSD_HEAD_DIM, (h + 1) * SSD_HEAD_DIM)
                xh = act[:, cols]
                dtx = (xh * dt[:, h:h + 1]).astype(BF16).astype(F32)
                s_new = decay[:, h:h + 1] * s0_ref[s, h] + jnp.transpose(dtx) * bg
                s_ref[s, h] = s_new
                yrow.append(_dot_nt(cg, s_new.astype(BF16)) + dsk_ref[:, cols] * xh)
        y = jnp.concatenate(yrow, axis=-1)
        y_ref[s] = _gated_group_rms(y, z_ref[s], nw_ref[...], SSD_GROUPS).astype(y_ref.dtype)


def _ssd_sample(z, xbc, dt, conv_prev, s0, cw, cb, dtb, alog, dsk, nw, heads):
    bs, width = z.shape
    cdim = xbc.shape[1]
    row = lambda n: pl.BlockSpec((SSD_SEQS, 1, n), lambda i: (i, 0, 0))
    full = lambda a: pl.BlockSpec(a.shape, lambda i: (0, 0))
    st = pl.BlockSpec((SSD_SEQS, heads, SSD_HEAD_DIM, SSD_STATE), lambda i: (i, 0, 0, 0))
    cv = pl.BlockSpec((SSD_SEQS, SSD_CONV - 1, cdim), lambda i: (i, 0, 0))
    y, s_new, cv_new = pl.pallas_call(
        functools.partial(_ssd_sample_kernel, heads=heads, width=width),
        grid=(bs // SSD_SEQS,),
        in_specs=[row(width), row(cdim), row(LANES), cv, full(cw), full(cb), full(dtb), full(alog), full(dsk),
                  full(nw), st],
        out_specs=[row(width), st, cv],
        out_shape=[jax.ShapeDtypeStruct((bs, 1, width), BF16),
                   jax.ShapeDtypeStruct(s0.shape, F32),
                   jax.ShapeDtypeStruct(conv_prev.shape, F32)],
        compiler_params=_cparams(("arbitrary",)),
        name="ssd_sample",
    )(z.reshape(bs, 1, width), xbc.reshape(bs, 1, cdim), dt.reshape(bs, 1, LANES), conv_prev, cw, cb, dtb, alog,
      dsk, nw, s0)
    return y.reshape(bs, width), s_new, cv_new


def _expert_kernel(be_ref, nu_ref, x_ref, w1_ref, b1_ref, w2_ref, b2_ref, o_ref, w1s, w2s, *, f):
    i = pl.program_id(0)
    prev = be_ref[jnp.maximum(i - 1, 0)]

    @pl.when((i == 0) | (be_ref[i] != prev))
    def _():
        w1s[...] = w1_ref[...].astype(BF16)
        w2s[...] = w2_ref[...].astype(BF16)

    @pl.when(i < nu_ref[0])
    def _():
        x = x_ref[...]
        acc = jnp.zeros(o_ref.shape, F32) + b2_ref[...]
        half = f // 2
        for c in range(2):
            glu = _dot(x, w1s[:, c * half:(c + 1) * half]) + b1_ref[:, c * half:(c + 1) * half]
            lin = _dot(x, w1s[:, f + c * half:f + (c + 1) * half]) + b1_ref[:, f + c * half:f + (c + 1) * half]
            glu = jnp.minimum(glu, SWIGLU_LIMIT)
            lin = jnp.clip(lin, -SWIGLU_LIMIT, SWIGLU_LIMIT)
            a = glu * jax.nn.sigmoid(SWIGLU_ALPHA * glu) * (lin + 1.0)
            acc = acc + _dot(a.astype(BF16), w2s[c * half:(c + 1) * half, :])
        o_ref[...] = acc

    @pl.when(i >= nu_ref[0])
    def _():
        o_ref[...] = jnp.zeros_like(o_ref)


def _experts(block_expert, n_used, xb, w1, b1, w2, b2):
    n_rows, d = xb.shape
    n_exp, _, f2 = w1.shape
    f = f2 // 2
    n_blocks = n_rows // MOE_ROWS
    grid_spec = pltpu.PrefetchScalarGridSpec(
        num_scalar_prefetch=2,
        grid=(n_blocks,),
        in_specs=[pl.BlockSpec((MOE_ROWS, d), lambda i, be, nu: (i, 0)),
                  pl.BlockSpec((None, d, f2), lambda i, be, nu: (be[i], 0, 0)),
                  pl.BlockSpec((None, 1, f2), lambda i, be, nu: (be[i], 0, 0)),
                  pl.BlockSpec((None, f, d), lambda i, be, nu: (be[i], 0, 0)),
                  pl.BlockSpec((None, 1, d), lambda i, be, nu: (be[i], 0, 0))],
        out_specs=pl.BlockSpec((MOE_ROWS, d), lambda i, be, nu: (i, 0)),
        scratch_shapes=[pltpu.VMEM((d, f2), BF16), pltpu.VMEM((f, d), BF16)],
    )
    return pl.pallas_call(
        functools.partial(_expert_kernel, f=f),
        grid_spec=grid_spec,
        out_shape=jax.ShapeDtypeStruct((n_rows, d), F32),
        compiler_params=_cparams(("arbitrary",)),
        name="moe_experts",
    )(block_expert, n_used, xb, w1, b1.reshape(n_exp, 1, f2), w2, b2.reshape(n_exp, 1, d))


def _route(logits, n_exp):
    t = logits.shape[0]
    top_val, top_idx = lax.top_k(logits, TOP_K)
    gate = jax.nn.softmax(top_val, axis=-1)
    n_assign = t * TOP_K
    flat_e = top_idx.reshape(n_assign)
    order = jnp.argsort(flat_e, stable=True)
    e_sorted = flat_e[order]
    counts = jnp.bincount(flat_e, length=n_exp)
    padded = (counts + MOE_ROWS - 1) // MOE_ROWS * MOE_ROWS
    pad_end = jnp.cumsum(padded)
    rank = jnp.arange(n_assign) - (jnp.cumsum(counts) - counts)[e_sorted]
    dest = ((pad_end - padded)[e_sorted] + rank).astype(jnp.int32)
    n_blocks = -(-n_assign // MOE_ROWS) + n_exp
    n_rows = n_blocks * MOE_ROWS
    row_tok = jnp.zeros((n_rows,), jnp.int32).at[dest].set((order // TOP_K).astype(jnp.int32))
    pos = jnp.zeros((n_assign,), jnp.int32).at[order].set(dest)
    block_expert = jnp.minimum(jnp.searchsorted(pad_end, jnp.arange(n_blocks) * MOE_ROWS, side="right"),
                               n_exp - 1).astype(jnp.int32)
    n_used = (pad_end[-1] // MOE_ROWS).astype(jnp.int32).reshape(1)
    return gate, row_tok, pos, block_expert, n_used


def _combine_kernel(x_ref, g_ref, gate_ref, y0, y1, y2, y3, *rest, final):
    gate = gate_ref[...]
    y = gate[:, 0:1] * y0[...]
    for j, yr in enumerate((y1, y2, y3), start=1):
        y = y + gate[:, j:j + 1] * yr[...]
    xn = x_ref[...] + g_ref[...] * y
    if final:
        fw_ref, o_ref = rest
        o_ref[...] = (xn * lax.rsqrt(jnp.mean(xn * xn, axis=-1, keepdims=True) + EPS)) * fw_ref[...]
    else:
        rest[0][...] = xn


def _combine(x, mod, kind, gate, yg, tok0, tiles_per_seq, final_w, name):
    t, d = x.shape
    tm = min(ROW_TILE, t)
    off = tok0 // tm
    (g_spec,) = _mod_specs(kind, (5,), tm, tiles_per_seq, d)
    in_specs = [pl.BlockSpec((tm, d), lambda i: (i, 0)), g_spec,
                pl.BlockSpec((tm, TOP_K), lambda i: (i + off, 0))]
    in_specs += [pl.BlockSpec((None, tm, d), lambda i, j=j: (j, i + off, 0)) for j in range(TOP_K)]
    args = [x, mod, gate, yg, yg, yg, yg]
    if final_w is not None:
        in_specs.append(pl.BlockSpec((1, d), lambda i: (0, 0)))
        args.append(final_w)
    return pl.pallas_call(
        functools.partial(_combine_kernel, final=final_w is not None),
        grid=(t // tm,),
        in_specs=in_specs,
        out_specs=pl.BlockSpec((tm, d), lambda i: (i, 0)),
        out_shape=jax.ShapeDtypeStruct((t, d), F32),
        compiler_params=_cparams(("arbitrary",)),
        name=name,
    )(*args)


def kernel(x_prompt, x_sample, cache_sb_k, cache_sb_v, page_table, state_ret, state_ssd, state_conv,
           c_prompt, c_sample, norm1_w, norm2_w, w_ada, b_ada, ab_w_in, ab_w_out, ret_gn_w, sb_bias,
           ssd_w_in, ssd_conv_w, ssd_conv_b, ssd_dt_bias, ssd_a_log, ssd_d, ssd_norm_w, ssd_w_out,
           router_w, router_b, moe_w1, moe_b1, moe_w2, moe_b2, final_norm_w):
    bp, sp, d = x_prompt.shape
    bs, ls, _ = x_sample.shape
    assert ls == 1 and sp % CHUNK == 0 and cache_sb_k.shape[2] == CHUNK
    depth = w_ada.shape[0]
    n_exp = router_w.shape[-1]
    tp = bp * sp
    past_len = page_table.shape[1] * cache_sb_k.shape[2]
    ssd_width = ssd_norm_w.shape[-1]
    ssd_heads = ssd_a_log.shape[-1]
    cdim = ssd_conv_w.shape[-1]
    tps = sp // min(ROW_TILE, tp)

    mod = _ada_mod(jnp.concatenate([c_prompt, c_sample], axis=0), w_ada, b_ada)
    xp = x_prompt.reshape(tp, d)
    xs = x_sample.reshape(bs, d)
    row1 = lambda a: a.reshape(1, -1)
    pad_l = lambda a: jnp.pad(a, ((0, 0), (0, LANES - a.shape[-1])))

    sbk_p, sbv_p, sbk_s, sbv_s, ret_p, ret_s = [], [], [], [], [], []
    ssd_p, ssd_s, conv_p, conv_s = [], [], [], []
    y_prompt = y_sample = None
    for layer in range(depth):
        i = layer // 2
        modp = mod[layer, :bp].reshape(bp * 6, 1, d)
        mods = mod[layer, bp:]
        nw1 = row1(norm1_w[layer])
        if layer % 2 == 0:
            w_in = ab_w_in[i].astype(BF16)
            ws = [w_in[:, :SB_WIDTH], w_in[:, SB_WIDTH:2 * SB_WIDTH], w_in[:, 2 * SB_WIDTH:3 * SB_WIDTH],
                  w_in[:, 3 * SB_WIDTH:]]
            plan = ((BF16,), (F32, BF16), (F32, BF16), (F32,))
            qa_p, ka_p, kab_p, va_p, vab_p, rt_p = _inproj(xp, nw1, modp, "p", ws, plan, tps, "ab_in_prompt")
            qa_s, ka_s, _, va_s, _, rt_s = _inproj(xs, nw1, mods, "s", ws, plan, 1, "ab_in_sample")
            gn = row1(ret_gn_w[i])
            o_sb_p = _sb_prompt(qa_p, kab_p, vab_p, sb_bias[i], bp, sp)
            o_rt_p, r_p = _ret_prompt(rt_p, gn, bp, sp)
            o_sb_s = _sb_sample(qa_s, cache_sb_k, cache_sb_v, i, page_table, sb_bias[i])
            o_rt_s, r_s = _ret_sample(rt_s, gn, state_ret[i], past_len)
            mixed_p = jnp.concatenate([o_sb_p, o_rt_p], axis=-1)
            mixed_s = jnp.concatenate([o_sb_s, o_rt_s], axis=-1)
            w_out = ab_w_out[i].astype(BF16)
            sbk_p.append(ka_p.reshape(bp, sp, SB_HEADS, SB_HEAD_DIM))
            sbv_p.append(va_p.reshape(bp, sp, SB_HEADS, SB_HEAD_DIM))
            sbk_s.append(ka_s.reshape(bs, 1, SB_HEADS, SB_HEAD_DIM))
            sbv_s.append(va_s.reshape(bs, 1, SB_HEADS, SB_HEAD_DIM))
            ret_p.append(r_p)
            ret_s.append(r_s)
        else:
            w_in = ssd_w_in[i]
            ws = [w_in[:, :ssd_width].astype(BF16), w_in[:, ssd_width:ssd_width + cdim].astype(BF16),
                  pad_l(w_in[:, ssd_width + cdim:]).astype(BF16)]
            plan = ((F32,), (F32,), (F32,))
            z_p, xbc_p, dt_p = _inproj(xp, nw1, modp, "p", ws, plan, tps, "ssd_in_prompt")
            z_s, xbc_s, dt_s = _inproj(xs, nw1, mods, "s", ws, plan, 1, "ssd_in_sample")
            consts = (ssd_conv_w[i], row1(ssd_conv_b[i]), pad_l(row1(ssd_dt_bias[i])), pad_l(row1(ssd_a_log[i])),
                      row1(jnp.repeat(ssd_d[i], SSD_HEAD_DIM)), row1(ssd_norm_w[i]))
            mixed_p, s_p, cv_p = _ssd_prompt(z_p, xbc_p, dt_p, *consts, bp, sp, ssd_heads)
            mixed_s, s_s, cv_s = _ssd_sample(z_s, xbc_s, dt_s, state_conv[i], state_ssd[i], *consts, ssd_heads)
            w_out = ssd_w_out[i].astype(BF16)
            ssd_p.append(s_p)
            ssd_s.append(s_s)
            conv_p.append(cv_p)
            conv_s.append(cv_s)

        rw = pad_l(router_w[layer])
        rwh = rw.astype(BF16)
        rwl = (rw - rwh.astype(F32)).astype(BF16)
        rb = pad_l(row1(router_b[layer]))
        nw2 = row1(norm2_w[layer])
        xp, h2p, lg_p = _post(mixed_p, w_out, xp, nw2, modp, "p", rwh, rwl, rb, tps, "post_prompt")
        xs, h2s, lg_s = _post(mixed_s, w_out, xs, nw2, mods, "s", rwh, rwl, rb, 1, "post_sample")

        logits = jnp.concatenate([lg_p, lg_s], axis=0)[:, :n_exp]
        gate, row_tok, pos, block_expert, n_used = _route(logits, n_exp)
        h2 = jnp.concatenate([h2p, h2s], axis=0)
        xb = jnp.take(h2, row_tok, axis=0)
        yb = _experts(block_expert, n_used, xb, moe_w1[layer], moe_b1[layer], moe_w2[layer], moe_b2[layer])
        yg = jnp.take(yb, pos.reshape(tp + bs, TOP_K).T, axis=0)
        last = layer == depth - 1
        fw = row1(final_norm_w) if last else None
        xp = _combine(xp, modp, "p", gate, yg, 0, tps, fw, "combine_prompt")
        xs = _combine(xs, mods, "s", gate, yg, tp, 1, fw, "combine_sample")

    st = lambda lst: jnp.stack(lst, axis=0)
    return (xp.reshape(bp, sp, d), xs.reshape(bs, ls, d), st(sbk_p), st(sbv_p), st(sbk_s), st(sbv_s),
            st(ret_p), st(ret_s), st(ssd_p), st(ssd_s), st(conv_p), st(conv_s))
```

```python
import functools
import math

import jax
import jax.numpy as jnp
from jax import lax
from jax.experimental import pallas as pl
from jax.experimental.pallas import tpu as pltpu

F32 = jnp.float32
BF16 = jnp.bfloat16

EPS = 1e-6
ROPE_BASE = 10000.0
SB_HEADS = 8
SB_HEAD_DIM = 64
SB_WIDTH = SB_HEADS * SB_HEAD_DIM
RET_HEADS = 4
RET_DIM = 128
RET_WIDTH = RET_HEADS * RET_DIM
SSD_HEAD_DIM = 64
SSD_GROUPS = 4
SSD_STATE = 128
SSD_CONV = 4
TOP_K = 4
SWIGLU_LIMIT = 7.0
SWIGLU_ALPHA = 1.702
CHUNK = 128
LANES = 128
ROW_TILE = 256
MOE_ROWS = 256
VMEM_LIMIT = 56 * 1024 * 1024


def _cparams(sem):
    return pltpu.CompilerParams(dimension_semantics=sem, vmem_limit_bytes=VMEM_LIMIT)


def _silu(x):
    return x * jax.nn.sigmoid(x)


def _softplus(x):
    return jnp.maximum(x, 0.0) + jnp.log1p(jnp.exp(-jnp.abs(x)))


def _modulate(x, nw, shift, scale):
    y = x * lax.rsqrt(jnp.mean(x * x, axis=-1, keepdims=True) + EPS)
    return (y * nw) * (1.0 + scale) + shift


def _split_bf16(x):
    hi = x.astype(BF16)
    lo = (x - hi.astype(F32)).astype(BF16)
    return hi, lo


def _dot(a, b):
    return jnp.dot(a, b, preferred_element_type=F32)


def _dot_nt(a, b):
    return lax.dot_general(a, b, (((1,), (1,)), ((), ())), preferred_element_type=F32)


def _dot_tn(a, b):
    return lax.dot_general(a, b, (((0,), (0,)), ((), ())), preferred_element_type=F32)


def _tri(n, strict, lower):
    r = lax.broadcasted_iota(jnp.int32, (n, n), 0)
    c = lax.broadcasted_iota(jnp.int32, (n, n), 1)
    if lower:
        m = (c < r) if strict else (c <= r)
    else:
        m = (c > r) if strict else (c >= r)
    return m


def _mod_specs(kind, ks, tm, tiles_per_seq, d):
    if kind == "p":
        return [pl.BlockSpec((None, 1, d), lambda i, k=k: ((i // tiles_per_seq) * 6 + k, 0, 0)) for k in ks]
    return [pl.BlockSpec((tm, d), lambda i, k=k: (i, k)) for k in ks]


def _ada_kernel(c_ref, w_ref, b_ref, o_ref):
    c = _silu(c_ref[...]).astype(BF16)
    o_ref[...] = _dot(c, w_ref[...].astype(BF16)) + b_ref[...]


def _ada_mod(c, w_ada, b_ada):
    depth, d, n = w_ada.shape
    rows = c.shape[0]
    tn = 1024
    return pl.pallas_call(
        _ada_kernel,
        grid=(depth, n // tn),
        in_specs=[pl.BlockSpec((rows, d), lambda l, j: (0, 0)),
                  pl.BlockSpec((None, d, tn), lambda l, j: (l, 0, j)),
                  pl.BlockSpec((None, 1, tn), lambda l, j: (l, 0, j))],
        out_specs=pl.BlockSpec((None, rows, tn), lambda l, j: (l, 0, j)),
        out_shape=jax.ShapeDtypeStruct((depth, rows, n), F32),
        compiler_params=_cparams(("arbitrary", "arbitrary")),
        name="ada_mod",
    )(c, w_ada, b_ada.reshape(depth, 1, n))


def _inproj_kernel(x_ref, nw_ref, sh_ref, sc_ref, *refs, plan):
    n_w = len(plan)
    w_refs, o_refs = refs[:n_w], refs[n_w:]
    h = _modulate(x_ref[...], nw_ref[...], sh_ref[...], sc_ref[...]).astype(BF16)
    oi = 0
    for w_ref, dts in zip(w_refs, plan):
        y = _dot(h, w_ref[...])
        for dt in dts:
            o_refs[oi][...] = y.astype(dt)
            oi += 1


def _inproj(x, nw, mod, kind, ws, plan, tiles_per_seq, name):
    t, d = x.shape
    tm = min(ROW_TILE, t)
    out_shape, out_specs = [], []
    for w, dts in zip(ws, plan):
        for dt in dts:
            out_shape.append(jax.ShapeDtypeStruct((t, w.shape[1]), dt))
            out_specs.append(pl.BlockSpec((tm, w.shape[1]), lambda i: (i, 0)))
    return pl.pallas_call(
        functools.partial(_inproj_kernel, plan=plan),
        grid=(t // tm,),
        in_specs=[pl.BlockSpec((tm, d), lambda i: (i, 0)),
                  pl.BlockSpec((1, d), lambda i: (0, 0))]
        + _mod_specs(kind, (0, 1), tm, tiles_per_seq, d)
        + [pl.BlockSpec(w.shape, lambda i: (0, 0)) for w in ws],
        out_specs=out_specs,
        out_shape=out_shape,
        compiler_params=_cparams(("arbitrary",)),
        name=name,
    )(x, nw, mod, mod, *ws)


ROUTE_GATE, ROUTE_IDX, ROUTE_RANK = 0, TOP_K, 2 * TOP_K
ROUTE_PAD_LOGIT = -1e30


def _route_tile(lg, cnt):
    tm = lg.shape[0]
    lane = lax.broadcasted_iota(jnp.int32, lg.shape, 1).astype(F32)
    work = lg
    vals, idxs = [], []
    for _ in range(TOP_K):
        m = jnp.max(work, axis=-1, keepdims=True)
        ix = jnp.min(jnp.where(work == m, lane, float(LANES)), axis=-1, keepdims=True)
        vals.append(m)
        idxs.append(ix)
        work = jnp.where(lane == ix, -jnp.inf, work)
    es = [jnp.exp(v - vals[0]) for v in vals]
    denom = es[0]
    for e in es[1:]:
        denom = denom + e
    onehot = jnp.zeros(lg.shape, F32)
    for ix in idxs:
        onehot = onehot + jnp.where(lane == ix, 1.0, 0.0)
    before = _tri(tm, strict=True, lower=True).astype(BF16)
    prefix = _dot(before, onehot.astype(BF16)) + cnt
    rec = jnp.zeros(lg.shape, F32)
    for j in range(TOP_K):
        rank = jnp.sum(jnp.where(lane == idxs[j], prefix, 0.0), axis=-1, keepdims=True)
        rec = jnp.where(lane == float(ROUTE_GATE + j), es[j] / denom, rec)
        rec = jnp.where(lane == float(ROUTE_IDX + j), idxs[j], rec)
        rec = jnp.where(lane == float(ROUTE_RANK + j), rank, rec)
    return rec, cnt + jnp.sum(onehot, axis=0, keepdims=True)


def _post_kernel(*refs, n_parts):
    m_refs, w_refs = refs[:n_parts], refs[n_parts:2 * n_parts]
    (x_ref, g_ref, nw_ref, sh_ref, sc_ref, rw_ref, rb_ref, cnt0_ref,
     xo_ref, h_ref, rt_ref, cnt_ref, cnt) = refs[2 * n_parts:]

    @pl.when(pl.program_id(0) == 0)
    def _():
        cnt[...] = cnt0_ref[...]

    o = _dot(m_refs[0][...], w_refs[0][...])
    for m_ref, w_ref in zip(m_refs[1:], w_refs[1:]):
        o = o + _dot(m_ref[...], w_ref[...])
    xn = x_ref[...] + g_ref[...] * o
    xo_ref[...] = xn
    h = _modulate(xn, nw_ref[...], sh_ref[...], sc_ref[...])
    h_ref[...] = h
    lg = _dot(h.astype(BF16), rw_ref[...]) + rb_ref[...]
    rec, cnt_new = _route_tile(lg, cnt[...])
    rt_ref[...] = rec
    cnt[...] = cnt_new
    cnt_ref[...] = cnt_new


def _post(parts, w_parts, x, nw2, mod, kind, rw, rb, cnt0, tiles_per_seq, name):
    t, d = x.shape
    tm = min(ROW_TILE, t)
    n_parts = len(parts)
    row = lambda n: pl.BlockSpec((tm, n), lambda i: (i, 0))
    full = lambda a: pl.BlockSpec(a.shape, lambda i: (0, 0))
    g_spec, sh_spec, sc_spec = _mod_specs(kind, (2, 3, 4), tm, tiles_per_seq, d)
    return pl.pallas_call(
        functools.partial(_post_kernel, n_parts=n_parts),
        grid=(t // tm,),
        in_specs=[row(p.shape[1]) for p in parts] + [full(w) for w in w_parts]
        + [row(d), g_spec, full(nw2), sh_spec, sc_spec, full(rw), full(rb), full(cnt0)],
        out_specs=[row(d), row(d), row(LANES), full(cnt0)],
        out_shape=[jax.ShapeDtypeStruct((t, d), F32), jax.ShapeDtypeStruct((t, d), F32),
                   jax.ShapeDtypeStruct((t, LANES), F32), jax.ShapeDtypeStruct(cnt0.shape, F32)],
        scratch_shapes=[pltpu.VMEM(cnt0.shape, F32)],
        compiler_params=_cparams(("arbitrary",)),
        name=name,
    )(*parts, *w_parts, x, mod, nw2, mod, mod, rw, rb, cnt0)


def _sb_prompt_kernel(bias_ref, q_ref, k_ref, v_ref, o_ref, acc_ref, run_ref):
    qi = pl.program_id(1)
    upper = _tri(CHUNK, strict=True, lower=True).astype(BF16)
    upper2 = jnp.concatenate([upper, upper], axis=0)
    causal = _tri(CHUNK, strict=True, lower=True)
    heads = range(SB_HEADS)
    cols = [slice(h * SB_HEAD_DIM, (h + 1) * SB_HEAD_DIM) for h in heads]

    def key_block(s0, diagonal):
        zs = [_dot_nt(q_ref[:, cols[h]], k_ref[pl.ds(s0, CHUNK), cols[h]]) for h in heads]
        ls, lks, splits = [], [], []
        for h in heads:
            z = zs[h] * (SB_HEAD_DIM ** -0.5) + bias_ref[h]
            sp = _softplus(z)
            lk = -sp
            if diagonal:
                lk = jnp.where(causal, lk, 0.0)
            ls.append(z - sp)
            lks.append(lk[:, 0:1])
            splits.append(jnp.concatenate(_split_bf16(lk), axis=1))
        betweens = [_dot(splits[h], upper2) for h in heads]
        ws = []
        for h in heads:
            between = betweens[h] if diagonal else betweens[h] + run_ref[h]
            w = jnp.exp(ls[h] + between)
            if diagonal:
                w = jnp.where(causal, w, 0.0)
            ws.append(w.astype(BF16))
            run_ref[h] = between[:, 0:1] + lks[h]
        outs = [_dot(ws[h], v_ref[pl.ds(s0, CHUNK), cols[h]]) for h in heads]
        for h in heads:
            if diagonal:
                acc_ref[:, cols[h]] = outs[h]
            else:
                acc_ref[:, cols[h]] += outs[h]

    key_block(pl.multiple_of(qi * CHUNK, CHUNK), True)

    def body(t, carry):
        key_block(pl.multiple_of((qi - t) * CHUNK, CHUNK), False)
        return carry

    lax.fori_loop(1, qi + 1, body, 0)
    o_ref[...] = acc_ref[...].astype(o_ref.dtype)


def _sb_prompt(q, k, v, bias, batch, seq):
    nq = seq // CHUNK
    return pl.pallas_call(
        _sb_prompt_kernel,
        grid=(batch, nq),
        in_specs=[pl.BlockSpec(memory_space=pltpu.SMEM),
                  pl.BlockSpec((CHUNK, SB_WIDTH), lambda b, i: (b * nq + i, 0)),
                  pl.BlockSpec((seq, SB_WIDTH), lambda b, i: (b, 0)),
                  pl.BlockSpec((seq, SB_WIDTH), lambda b, i: (b, 0))],
        out_specs=pl.BlockSpec((CHUNK, SB_WIDTH), lambda b, i: (b * nq + i, 0)),
        out_shape=jax.ShapeDtypeStruct((batch * seq, SB_WIDTH), BF16),
        scratch_shapes=[pltpu.VMEM((CHUNK, SB_WIDTH), F32), pltpu.VMEM((SB_HEADS, CHUNK, 1), F32)],
        compiler_params=_cparams(("arbitrary", "arbitrary")),
        name="sb_prompt",
    )(bias, q, k, v)


def _sb_sample_kernel(pt_ref, q_ref, bias_ref, *refs, n_pages):
    k_refs, v_refs, o_ref = refs[:n_pages], refs[n_pages:2 * n_pages], refs[2 * n_pages]
    row = lax.broadcasted_iota(jnp.int32, (SB_HEADS, SB_WIDTH), 0)
    col = lax.broadcasted_iota(jnp.int32, (SB_HEADS, SB_WIDTH), 1)
    own = (col // SB_HEAD_DIM) == row
    qm = jnp.where(own, jnp.broadcast_to(q_ref[...].astype(F32), (SB_HEADS, SB_WIDTH)), 0.0).astype(BF16)
    upper = _tri(CHUNK, strict=True, lower=True).astype(BF16)
    upper2 = jnp.concatenate([upper, upper], axis=0)
    bias = bias_ref[...]
    pages = range(n_pages)
    zs = [_dot_nt(qm, k_refs[p][...].astype(BF16)) * (SB_HEAD_DIM ** -0.5) + bias for p in pages]
    sps = [_softplus(z) for z in zs]
    splits = [jnp.concatenate(_split_bf16(-sp), axis=1) for sp in sps]
    betweens = [_dot(s, upper2) for s in splits]
    run = jnp.zeros((SB_HEADS, 1), F32)
    acc = jnp.zeros((SB_HEADS, SB_WIDTH), F32)
    for p in reversed(pages):
        between = betweens[p] + run
        w = jnp.exp((zs[p] - sps[p]) + between)
        acc = acc + _dot(w.astype(BF16), v_refs[p][...].astype(BF16))
        run = between[:, 0:1] - sps[p][:, 0:1]
    o_ref[...] = jnp.sum(jnp.where(own, acc, 0.0), axis=0, keepdims=True).astype(o_ref.dtype)


def _sb_sample(q, cache_k, cache_v, layer, page_table, bias):
    bs, n_pages = page_table.shape
    page = cache_k.shape[2]
    ck = cache_k.reshape(cache_k.shape[0], cache_k.shape[1], page, SB_WIDTH)
    cv = cache_v.reshape(cache_v.shape[0], cache_v.shape[1], page, SB_WIDTH)
    page_spec = lambda p: pl.BlockSpec((None, None, page, SB_WIDTH),
                                       lambda b, pt, p=p: (layer, pt[b * n_pages + p], 0, 0))
    grid_spec = pltpu.PrefetchScalarGridSpec(
        num_scalar_prefetch=1,
        grid=(bs,),
        in_specs=[pl.BlockSpec((None, 1, SB_WIDTH), lambda b, pt: (b, 0, 0)),
                  pl.BlockSpec((SB_HEADS, 1), lambda b, pt: (0, 0))]
        + [page_spec(p) for p in range(n_pages)] * 2,
        out_specs=pl.BlockSpec((None, 1, SB_WIDTH), lambda b, pt: (b, 0, 0)),
    )
    out = pl.pallas_call(
        functools.partial(_sb_sample_kernel, n_pages=n_pages),
        grid_spec=grid_spec,
        out_shape=jax.ShapeDtypeStruct((bs, 1, SB_WIDTH), BF16),
        compiler_params=_cparams(("arbitrary",)),
        name="sb_sample",
    )(page_table.reshape(-1), q.reshape(bs, 1, SB_WIDTH), bias.reshape(SB_HEADS, 1),
      *([ck] * n_pages), *([cv] * n_pages))
    return out.reshape(bs, SB_WIDTH)


def _log_gamma(h):
    return math.log1p(-(2.0 ** (-5.0 - h)))


def _rope_tables(pos):
    half = RET_DIM // 2
    inv = ROPE_BASE ** (-jnp.arange(half, dtype=F32) / half)
    ang = pos.astype(F32)[:, None] * inv[None, :]
    cos, sin = jnp.cos(ang), jnp.sin(ang)
    return jnp.concatenate([cos, cos], axis=-1), jnp.concatenate([-sin, sin], axis=-1)


def _rope(x, cos2, sin2):
    return x * cos2 + pltpu.roll(x, RET_DIM // 2, 1) * sin2


def _group_norm_gate(o, gn, g):
    mu = jnp.mean(o, axis=-1, keepdims=True)
    c = o - mu
    var = jnp.mean(c * c, axis=-1, keepdims=True)
    return _silu(g) * ((c * lax.rsqrt(var + EPS)) * gn)


def _ret_prompt_kernel(q_ref, k_ref, v_ref, g_ref, cos_ref, sin_ref, gn_ref, o_ref, r_ref, state):
    c = pl.program_id(1)

    @pl.when(c == 0)
    def _():
        state[...] = jnp.zeros_like(state)

    cos2, sin2 = cos_ref[...], sin_ref[...]
    li = lax.broadcasted_iota(jnp.int32, (CHUNK, CHUNK), 0)
    mi = lax.broadcasted_iota(jnp.int32, (CHUNK, CHUNK), 1)
    diff = li - mi
    idx = lax.broadcasted_iota(jnp.int32, (CHUNK, 1), 0).astype(F32)
    for h in range(RET_HEADS):
        lg = _log_gamma(h)
        cols = slice(h * RET_DIM, (h + 1) * RET_DIM)
        q = _rope(q_ref[:, cols], cos2, sin2)
        k = _rope(k_ref[:, cols], cos2, sin2) * (RET_DIM ** -0.5)
        v = v_ref[:, cols].astype(BF16)
        qb = q.astype(BF16)
        decay = jnp.where(diff >= 0, jnp.exp(jnp.maximum(diff, 0).astype(F32) * lg), 0.0)
        scores = _dot_nt(qb, k.astype(BF16)) * decay
        inner = _dot(scores.astype(BF16), v)
        r_old = state[h]
        from_state = _dot(qb, r_old.astype(BF16)) * jnp.exp((idx + 1.0) * lg)
        k_dec = k * jnp.exp((CHUNK - 1.0 - idx) * lg)
        state[h] = math.exp(CHUNK * lg) * r_old + _dot_tn(k_dec.astype(BF16), v)
        o = _group_norm_gate(inner + from_state, gn_ref[:, cols], g_ref[:, cols])
        o_ref[:, cols] = o.astype(o_ref.dtype)

    @pl.when(c == pl.num_programs(1) - 1)
    def _():
        r_ref[...] = state[...]


def _ret_prompt(ret, gn_w, batch, seq):
    nc = seq // CHUNK
    cos2, sin2 = _rope_tables(jnp.arange(seq, dtype=jnp.int32))
    col = lambda j: pl.BlockSpec((CHUNK, RET_WIDTH), lambda b, c, j=j: (b * nc + c, j))
    tab = pl.BlockSpec((CHUNK, RET_DIM), lambda b, c: (c, 0))
    return pl.pallas_call(
        _ret_prompt_kernel,
        grid=(batch, nc),
        in_specs=[col(0), col(1), col(2), col(3), tab, tab, pl.BlockSpec((1, RET_WIDTH), lambda b, c: (0, 0))],
        out_specs=[pl.BlockSpec((CHUNK, RET_WIDTH), lambda b, c: (b * nc + c, 0)),
                   pl.BlockSpec((None, RET_HEADS, RET_DIM, RET_DIM), lambda b, c: (b, 0, 0, 0))],
        out_shape=[jax.ShapeDtypeStruct((batch * seq, RET_WIDTH), BF16),
                   jax.ShapeDtypeStruct((batch, RET_HEADS, RET_DIM, RET_DIM), F32)],
        scratch_shapes=[pltpu.VMEM((RET_HEADS, RET_DIM, RET_DIM), F32)],
        compiler_params=_cparams(("arbitrary", "arbitrary")),
        name="ret_prompt",
    )(ret, ret, ret, ret, cos2, sin2, gn_w)


RET_SEQS = 8


def _ret_sample_kernel(q_ref, k_ref, v_ref, g_ref, cos_ref, sin_ref, gn_ref, r0_ref, o_ref, r_ref):
    cos2, sin2 = cos_ref[...], sin_ref[...]
    for h in range(RET_HEADS):
        gamma = math.exp(_log_gamma(h))
        cols = slice(h * RET_DIM, (h + 1) * RET_DIM)
        q = _rope(q_ref[:, cols], cos2, sin2)
        k = _rope(k_ref[:, cols], cos2, sin2) * (RET_DIM ** -0.5)
        v = v_ref[:, cols]
        qb, kb, vb = q.astype(BF16), k.astype(BF16), v.astype(BF16)
        qk = jnp.sum(qb.astype(F32) * kb.astype(F32), axis=-1, keepdims=True)
        inner = qk * vb.astype(F32)
        rows = []
        for s in range(RET_SEQS):
            r_old = r0_ref[s, h]
            rows.append(_dot(qb[s:s + 1, :], r_old.astype(BF16)) * gamma)
            k_col = jnp.transpose(kb[s:s + 1, :].astype(F32))
            r_ref[s, h] = gamma * r_old + k_col * vb[s:s + 1, :].astype(F32)
        o = inner + jnp.concatenate(rows, axis=0)
        o_ref[:, cols] = _group_norm_gate(o, gn_ref[:, cols], g_ref[:, cols]).astype(o_ref.dtype)


def _ret_sample(ret, gn_w, r0, layer, past_len):
    bs = ret.shape[0]
    cos2, sin2 = _rope_tables(jnp.full((1,), past_len, jnp.int32))
    col = lambda j: pl.BlockSpec((RET_SEQS, RET_WIDTH), lambda i, j=j: (i, j))
    one = lambda n: pl.BlockSpec((1, n), lambda i: (0, 0))
    st_in = pl.BlockSpec((None, RET_SEQS, RET_HEADS, RET_DIM, RET_DIM), lambda i: (layer, i, 0, 0, 0))
    st = pl.BlockSpec((RET_SEQS, RET_HEADS, RET_DIM, RET_DIM), lambda i: (i, 0, 0, 0))
    return pl.pallas_call(
        _ret_sample_kernel,
        grid=(bs // RET_SEQS,),
        in_specs=[col(0), col(1), col(2), col(3), one(RET_DIM), one(RET_DIM), one(RET_WIDTH), st_in],
        out_specs=[pl.BlockSpec((RET_SEQS, RET_WIDTH), lambda i: (i, 0)), st],
        out_shape=[jax.ShapeDtypeStruct((bs, RET_WIDTH), BF16),
                   jax.ShapeDtypeStruct(r0.shape[1:], F32)],
        compiler_params=_cparams(("arbitrary",)),
        name="ret_sample",
    )(ret, ret, ret, ret, cos2, sin2, gn_w, r0)


def _gated_group_rms(y, z, nw, groups):
    yg = y * _silu(z)
    gw = y.shape[-1] // groups
    outs = []
    for g in range(groups):
        a = yg[:, g * gw:(g + 1) * gw]
        outs.append(a * lax.rsqrt(jnp.mean(a * a, axis=-1, keepdims=True) + EPS))
    return jnp.concatenate(outs, axis=-1) * nw


def _ssd_prompt_kernel(z_ref, xbc_ref, dt_ref, cw_ref, cb_ref, dtb_ref, alog_ref, dsk_ref, nw_ref,
                       y_ref, s_ref, cv_ref, state, win, ybuf, *, heads, width):
    c = pl.program_id(1)
    hpg = heads // SSD_GROUPS
    gn = SSD_GROUPS * SSD_STATE

    @pl.when(c == 0)
    def _():
        state[...] = jnp.zeros_like(state)
        win[0:8, :] = jnp.zeros((8, win.shape[1]), F32)

    win[8:8 + CHUNK, :] = xbc_ref[...]
    conv = cb_ref[...]
    for j in range(SSD_CONV):
        conv = conv + win[5 + j:5 + j + CHUNK, :] * cw_ref[j:j + 1, :]
    act = _silu(conv)

    @pl.when(c == pl.num_programs(1) - 1)
    def _():
        cv_ref[...] = win[CHUNK + 5:CHUNK + 8, :]

    win[0:8, :] = win[CHUNK:CHUNK + 8, :]

    dt = _softplus(dt_ref[...] + dtb_ref[...])
    a = -jnp.exp(alog_ref[...])
    da_hi, da_lo = _split_bf16(dt * a)
    lower = _tri(CHUNK, strict=False, lower=True).astype(BF16)
    acs = _dot(lower, da_hi) + _dot(lower, da_lo)
    acs_t = jnp.transpose(acs)
    dt_t = jnp.transpose(dt)
    causal = _tri(CHUNK, strict=False, lower=True)
    for g in range(SSD_GROUPS):
        bg = act[:, width + g * SSD_STATE:width + (g + 1) * SSD_STATE].astype(BF16)
        cg = act[:, width + gn + g * SSD_STATE:width + gn + (g + 1) * SSD_STATE].astype(BF16)
        cbm = _dot_nt(cg, bg)
        for r in range(hpg):
            h = g * hpg + r
            cols = slice(h * SSD_HEAD_DIM, (h + 1) * SSD_HEAD_DIM)
            xh = act[:, cols]
            acs_col = acs[:, h:h + 1]
            seg = jnp.where(causal, acs_col - acs_t[h:h + 1, :], -jnp.inf)
            w = cbm * jnp.exp(seg) * dt_t[h:h + 1, :]
            s_old = state[h]
            y = _dot(w.astype(BF16), xh.astype(BF16))
            y = y + _dot_nt(cg, s_old.astype(BF16)) * jnp.exp(acs_col)
            last = acs[CHUNK - 1:CHUNK, h:h + 1]
            to_end = jnp.exp(last - acs_col) * dt[:, h:h + 1]
            state[h] = jnp.exp(last) * s_old + _dot_tn((xh * to_end).astype(BF16), bg)
            ybuf[:, cols] = y + dsk_ref[:, cols] * xh

    y_ref[...] = _gated_group_rms(ybuf[...], z_ref[...], nw_ref[...], SSD_GROUPS).astype(y_ref.dtype)

    @pl.when(c == pl.num_programs(1) - 1)
    def _():
        s_ref[...] = state[...]


def _ssd_prompt(z, xbc, dt, cw, cb, dtb, alog, dsk, nw, batch, seq, heads):
    nc = seq // CHUNK
    width = z.shape[1]
    cdim = xbc.shape[1]
    row = lambda n: pl.BlockSpec((CHUNK, n), lambda b, c: (b * nc + c, 0))
    full = lambda a: pl.BlockSpec(a.shape, lambda b, c: (0, 0))
    return pl.pallas_call(
        functools.partial(_ssd_prompt_kernel, heads=heads, width=width),
        grid=(batch, nc),
        in_specs=[row(width), row(cdim), row(LANES), full(cw), full(cb), full(dtb), full(alog), full(dsk), full(nw)],
        out_specs=[row(width),
                   pl.BlockSpec((None, heads, SSD_HEAD_DIM, SSD_STATE), lambda b, c: (b, 0, 0, 0)),
                   pl.BlockSpec((None, SSD_CONV - 1, cdim), lambda b, c: (b, 0, 0))],
        out_shape=[jax.ShapeDtypeStruct((batch * seq, width), BF16),
                   jax.ShapeDtypeStruct((batch, heads, SSD_HEAD_DIM, SSD_STATE), F32),
                   jax.ShapeDtypeStruct((batch, SSD_CONV - 1, cdim), F32)],
        scratch_shapes=[pltpu.VMEM((heads, SSD_HEAD_DIM, SSD_STATE), F32),
                        pltpu.VMEM((CHUNK + 8, cdim), F32),
                        pltpu.VMEM((CHUNK, width), F32)],
        compiler_params=_cparams(("arbitrary", "arbitrary")),
        name="ssd_prompt",
    )(z, xbc, dt, cw, cb, dtb, alog, dsk, nw)


SSD_SEQS = 4


def _ssd_sample_kernel(z_ref, xbc_ref, dt_ref, cprev_ref, cw_ref, cb_ref, dtb_ref, alog_ref, dsk_ref, nw_ref,
                       s0_ref, y_ref, s_ref, cv_ref, *, heads, width):
    hpg = heads // SSD_GROUPS
    gn = SSD_GROUPS * SSD_STATE
    for s in range(SSD_SEQS):
        dt = _softplus(dt_ref[s] + dtb_ref[...])
        decay = jnp.exp(dt * (-jnp.exp(alog_ref[...])))
        xrow = xbc_ref[s]
        prev = cprev_ref[s]
        conv = cb_ref[...] + xrow * cw_ref[SSD_CONV - 1:SSD_CONV, :]
        for j in range(SSD_CONV - 1):
            conv = conv + prev[j:j + 1, :] * cw_ref[j:j + 1, :]
        cv_ref[s, 0:SSD_CONV - 2, :] = prev[1:SSD_CONV - 1, :]
        cv_ref[s, SSD_CONV - 2:SSD_CONV - 1, :] = xrow
        act = _silu(conv)
        yrow = []
        for g in range(SSD_GROUPS):
            bg = act[:, width + g * SSD_STATE:width + (g + 1) * SSD_STATE].astype(BF16).astype(F32)
            cg = act[:, width + gn + g * SSD_STATE:width + gn + (g + 1) * SSD_STATE].astype(BF16)
            for r in range(hpg):
                h = g * hpg + r
                cols = slice(h * SSD_HEAD_DIM, (h + 1) * SSD_HEAD_DIM)
                xh = act[:, cols]
                dtx = (xh * dt[:, h:h + 1]).astype(BF16).astype(F32)
                s_new = decay[:, h:h + 1] * s0_ref[s, h] + jnp.transpose(dtx) * bg
                s_ref[s, h] = s_new
                yrow.append(_dot_nt(cg, s_new.astype(BF16)) + dsk_ref[:, cols] * xh)
        y = jnp.concatenate(yrow, axis=-1)
        y_ref[s] = _gated_group_rms(y, z_ref[s], nw_ref[...], SSD_GROUPS).astype(y_ref.dtype)


def _ssd_sample(z, xbc, dt, conv_prev, s0, layer, cw, cb, dtb, alog, dsk, nw, heads):
    bs, width = z.shape
    cdim = xbc.shape[1]
    row = lambda n: pl.BlockSpec((SSD_SEQS, 1, n), lambda i: (i, 0, 0))
    full = lambda a: pl.BlockSpec(a.shape, lambda i: (0, 0))
    st_in = pl.BlockSpec((None, SSD_SEQS, heads, SSD_HEAD_DIM, SSD_STATE), lambda i: (layer, i, 0, 0, 0))
    cv_in = pl.BlockSpec((None, SSD_SEQS, SSD_CONV - 1, cdim), lambda i: (layer, i, 0, 0))
    st = pl.BlockSpec((SSD_SEQS, heads, SSD_HEAD_DIM, SSD_STATE), lambda i: (i, 0, 0, 0))
    cv = pl.BlockSpec((SSD_SEQS, SSD_CONV - 1, cdim), lambda i: (i, 0, 0))
    y, s_new, cv_new = pl.pallas_call(
        functools.partial(_ssd_sample_kernel, heads=heads, width=width),
        grid=(bs // SSD_SEQS,),
        in_specs=[row(width), row(cdim), row(LANES), cv_in, full(cw), full(cb), full(dtb), full(alog), full(dsk),
                  full(nw), st_in],
        out_specs=[row(width), st, cv],
        out_shape=[jax.ShapeDtypeStruct((bs, 1, width), BF16),
                   jax.ShapeDtypeStruct(s0.shape[1:], F32),
                   jax.ShapeDtypeStruct(conv_prev.shape[1:], F32)],
        compiler_params=_cparams(("arbitrary",)),
        name="ssd_sample",
    )(z.reshape(bs, 1, width), xbc.reshape(bs, 1, cdim), dt.reshape(bs, 1, LANES), conv_prev, cw, cb, dtb, alog,
      dsk, nw, s0)
    return y.reshape(bs, width), s_new, cv_new


def _expert_kernel(be_ref, nu_ref, x_ref, w1_ref, b1_ref, w2_ref, b2_ref, o_ref, w1s, w2s, *, f):
    i = pl.program_id(0)
    prev = be_ref[jnp.maximum(i - 1, 0)]

    @pl.when((i == 0) | (be_ref[i] != prev))
    def _():
        w1s[...] = w1_ref[...].astype(BF16)
        w2s[...] = w2_ref[...].astype(BF16)

    @pl.when(i < nu_ref[0])
    def _():
        x = x_ref[...].astype(BF16)
        acc = jnp.zeros(o_ref.shape, F32) + b2_ref[...]
        half = f // 2
        for c in range(2):
            glu = _dot(x, w1s[:, c * half:(c + 1) * half]) + b1_ref[:, c * half:(c + 1) * half]
            lin = _dot(x, w1s[:, f + c * half:f + (c + 1) * half]) + b1_ref[:, f + c * half:f + (c + 1) * half]
            glu = jnp.minimum(glu, SWIGLU_LIMIT)
            lin = jnp.clip(lin, -SWIGLU_LIMIT, SWIGLU_LIMIT)
            a = glu * jax.nn.sigmoid(SWIGLU_ALPHA * glu) * (lin + 1.0)
            acc = acc + _dot(a.astype(BF16), w2s[c * half:(c + 1) * half, :])
        o_ref[...] = acc

    @pl.when(i >= nu_ref[0])
    def _():
        o_ref[...] = jnp.zeros_like(o_ref)


def _experts(block_expert, n_used, xb, layer, w1, b1, w2, b2):
    n_rows = xb.shape[0]
    _, n_exp, d, f2 = w1.shape
    f = f2 // 2
    n_blocks = n_rows // MOE_ROWS
    rows = pl.BlockSpec((MOE_ROWS, d), lambda i, be, nu: (i, 0))
    grid_spec = pltpu.PrefetchScalarGridSpec(
        num_scalar_prefetch=2,
        grid=(n_blocks,),
        in_specs=[rows,
                  pl.BlockSpec((None, None, d, f2), lambda i, be, nu: (layer, be[i], 0, 0)),
                  pl.BlockSpec((None, None, 1, f2), lambda i, be, nu: (layer, be[i], 0, 0)),
                  pl.BlockSpec((None, None, f, d), lambda i, be, nu: (layer, be[i], 0, 0)),
                  pl.BlockSpec((None, None, 1, d), lambda i, be, nu: (layer, be[i], 0, 0))],
        out_specs=rows,
        scratch_shapes=[pltpu.VMEM((d, f2), BF16), pltpu.VMEM((f, d), BF16)],
    )
    depth = w1.shape[0]
    return pl.pallas_call(
        functools.partial(_expert_kernel, f=f),
        grid_spec=grid_spec,
        out_shape=jax.ShapeDtypeStruct(xb.shape, F32),
        compiler_params=_cparams(("arbitrary",)),
        name="moe_experts",
    )(block_expert, n_used, xb, w1, b1.reshape(depth, n_exp, 1, f2), w2, b2.reshape(depth, n_exp, 1, d))


def _expert_layout(counts, n_blocks):
    n_exp = counts.shape[0]
    padded = (counts + MOE_ROWS - 1) // MOE_ROWS * MOE_ROWS
    pad_end = jnp.cumsum(padded)
    starts = jnp.arange(n_blocks, dtype=jnp.int32) * MOE_ROWS
    block_expert = jnp.minimum(jnp.sum(starts[:, None] >= pad_end[None, :], axis=1), n_exp - 1).astype(jnp.int32)
    return pad_end - padded, block_expert, (pad_end[-1] // MOE_ROWS).astype(jnp.int32).reshape(1)


def _row_positions(rec, pad_start):
    n_exp = pad_start.shape[0]
    idx = rec[:, ROUTE_IDX:ROUTE_IDX + TOP_K].astype(jnp.int32)
    rank = rec[:, ROUTE_RANK:ROUTE_RANK + TOP_K].astype(jnp.int32)
    start = jnp.sum(jnp.where(idx[..., None] == jnp.arange(n_exp, dtype=jnp.int32), pad_start, 0), axis=-1)
    return (start + rank).reshape(-1).astype(jnp.int32)


DMA_UNROLL = 8
COMBINE_ROWS = 32


def _dispatch_kernel(pos_ref, h_ref, xb_in, xb_ref, sem):
    del xb_in
    tm = h_ref.shape[0]
    base = pl.program_id(0) * (tm * TOP_K)

    def row_copy(t, j):
        dst = pos_ref[base + t * TOP_K + j]
        return pltpu.make_async_copy(h_ref.at[pl.ds(t, 1), :], xb_ref.at[pl.ds(dst, 1), :], sem)

    def issue(t, carry):
        for j in range(TOP_K):
            row_copy(t, j).start()
        return carry

    def drain(t, carry):
        for j in range(TOP_K):
            row_copy(t, j).wait()
        return carry

    lax.fori_loop(0, tm, issue, 0, unroll=DMA_UNROLL)
    lax.fori_loop(0, tm, drain, 0, unroll=DMA_UNROLL)


def _dispatch(pos, h, xb, name):
    t, d = h.shape
    tm = min(ROW_TILE, t)
    grid_spec = pltpu.PrefetchScalarGridSpec(
        num_scalar_prefetch=1,
        grid=(t // tm,),
        in_specs=[pl.BlockSpec((tm, d), lambda i, p: (i, 0)),
                  pl.BlockSpec(memory_space=pl.ANY)],
        out_specs=pl.BlockSpec(memory_space=pl.ANY),
        scratch_shapes=[pltpu.SemaphoreType.DMA],
    )
    return pl.pallas_call(
        _dispatch_kernel,
        grid_spec=grid_spec,
        out_shape=jax.ShapeDtypeStruct(xb.shape, xb.dtype),
        input_output_aliases={2: 0},
        compiler_params=_cparams(("arbitrary",)),
        name=name,
    )(pos, h, xb)


def _combine_kernel(pos_ref, x_ref, g_ref, rec_ref, yb_ref, *rest, final):
    if final:
        fw_ref, o_ref, ybuf, sem = rest
    else:
        o_ref, ybuf, sem = rest
    tm = x_ref.shape[0]
    base = pl.program_id(0) * (tm * TOP_K)

    def row_copy(t, j):
        src = pos_ref[base + t * TOP_K + j]
        return pltpu.make_async_copy(yb_ref.at[pl.ds(src, 1), :], ybuf.at[j, pl.ds(t, 1), :], sem)

    def issue(t, carry):
        for j in range(TOP_K):
            row_copy(t, j).start()
        return carry

    def drain(t, carry):
        for j in range(TOP_K):
            row_copy(t, j).wait()
        return carry

    lax.fori_loop(0, tm, issue, 0, unroll=DMA_UNROLL)
    lax.fori_loop(0, tm, drain, 0, unroll=DMA_UNROLL)

    def rows(c, carry):
        r = pl.ds(pl.multiple_of(c * COMBINE_ROWS, COMBINE_ROWS), COMBINE_ROWS)
        rec = rec_ref[r, :]
        y = rec[:, ROUTE_GATE:ROUTE_GATE + 1] * ybuf[0, r, :]
        for j in range(1, TOP_K):
            y = y + rec[:, ROUTE_GATE + j:ROUTE_GATE + j + 1] * ybuf[j, r, :]
        g = g_ref[...] if g_ref.shape[0] == 1 else g_ref[r, :]
        xn = x_ref[r, :] + g * y
        if final:
            xn = (xn * lax.rsqrt(jnp.mean(xn * xn, axis=-1, keepdims=True) + EPS)) * fw_ref[...]
        o_ref[r, :] = xn
        return carry

    lax.fori_loop(0, tm // COMBINE_ROWS, rows, 0)


def _combine(pos, x, mod, kind, rec, yb, tiles_per_seq, final_w, name):
    t, d = x.shape
    tm = min(ROW_TILE, t)
    (g_spec,) = _mod_specs(kind, (5,), tm, tiles_per_seq, d)
    with_p = lambda spec: pl.BlockSpec(spec.block_shape, lambda i, p, f=spec.index_map: f(i))
    in_specs = [pl.BlockSpec((tm, d), lambda i, p: (i, 0)), with_p(g_spec),
                pl.BlockSpec((tm, LANES), lambda i, p: (i, 0)),
                pl.BlockSpec(memory_space=pl.ANY)]
    args = [pos, x, mod, rec, yb]
    if final_w is not None:
        in_specs.append(pl.BlockSpec((1, d), lambda i, p: (0, 0)))
        args.append(final_w)
    grid_spec = pltpu.PrefetchScalarGridSpec(
        num_scalar_prefetch=1,
        grid=(t // tm,),
        in_specs=in_specs,
        out_specs=pl.BlockSpec((tm, d), lambda i, p: (i, 0)),
        scratch_shapes=[pltpu.VMEM((TOP_K, tm, d), F32), pltpu.SemaphoreType.DMA],
    )
    return pl.pallas_call(
        functools.partial(_combine_kernel, final=final_w is not None),
        grid_spec=grid_spec,
        out_shape=jax.ShapeDtypeStruct((t, d), F32),
        compiler_params=_cparams(("arbitrary",)),
        name=name,
    )(*args)


def kernel(x_prompt, x_sample, cache_sb_k, cache_sb_v, page_table, state_ret, state_ssd, state_conv,
           c_prompt, c_sample, norm1_w, norm2_w, w_ada, b_ada, ab_w_in, ab_w_out, ret_gn_w, sb_bias,
           ssd_w_in, ssd_conv_w, ssd_conv_b, ssd_dt_bias, ssd_a_log, ssd_d, ssd_norm_w, ssd_w_out,
           router_w, router_b, moe_w1, moe_b1, moe_w2, moe_b2, final_norm_w):
    bp, sp, d = x_prompt.shape
    bs, ls, _ = x_sample.shape
    assert ls == 1 and sp % CHUNK == 0 and cache_sb_k.shape[2] == CHUNK
    depth = w_ada.shape[0]
    n_exp = router_w.shape[-1]
    tp = bp * sp
    past_len = page_table.shape[1] * cache_sb_k.shape[2]
    ssd_width = ssd_norm_w.shape[-1]
    ssd_heads = ssd_a_log.shape[-1]
    cdim = ssd_conv_w.shape[-1]
    tps = sp // min(ROW_TILE, tp)
    assert d % LANES == 0 and n_exp <= LANES
    n_blocks = -(-((tp + bs) * TOP_K) // MOE_ROWS) + n_exp

    mod = _ada_mod(jnp.concatenate([c_prompt, c_sample], axis=0), w_ada, b_ada)
    xp = x_prompt.reshape(tp, d)
    xs = x_sample.reshape(bs, d)
    row1 = lambda a: a.reshape(1, -1)
    pad_l = lambda a: jnp.pad(a, ((0, 0), (0, LANES - a.shape[-1])))

    sbk_p, sbv_p, sbk_s, sbv_s, ret_p, ret_s = [], [], [], [], [], []
    ssd_p, ssd_s, conv_p, conv_s = [], [], [], []
    y_prompt = y_sample = None
    for layer in range(depth):
        i = layer // 2
        modp = mod[layer, :bp].reshape(bp * 6, 1, d)
        mods = mod[layer, bp:]
        nw1 = row1(norm1_w[layer])
        if layer % 2 == 0:
            w_in = ab_w_in[i].astype(BF16)
            ws = [w_in[:, :SB_WIDTH], w_in[:, SB_WIDTH:2 * SB_WIDTH], w_in[:, 2 * SB_WIDTH:3 * SB_WIDTH],
                  w_in[:, 3 * SB_WIDTH:]]
            plan = ((BF16,), (F32, BF16), (F32, BF16), (F32,))
            qa_p, ka_p, kab_p, va_p, vab_p, rt_p = _inproj(xp, nw1, modp, "p", ws, plan, tps, "ab_in_prompt")
            qa_s, ka_s, _, va_s, _, rt_s = _inproj(xs, nw1, mods, "s", ws, plan, 1, "ab_in_sample")
            gn = row1(ret_gn_w[i])
            o_sb_p = _sb_prompt(qa_p, kab_p, vab_p, sb_bias[i], bp, sp)
            o_rt_p, r_p = _ret_prompt(rt_p, gn, bp, sp)
            o_sb_s = _sb_sample(qa_s, cache_sb_k, cache_sb_v, i, page_table, sb_bias[i])
            o_rt_s, r_s = _ret_sample(rt_s, gn, state_ret, i, past_len)
            parts_p, parts_s = [o_sb_p, o_rt_p], [o_sb_s, o_rt_s]
            w_out = ab_w_out[i].astype(BF16)
            w_parts = [w_out[:SB_WIDTH], w_out[SB_WIDTH:]]
            sbk_p.append(ka_p.reshape(bp, sp, SB_HEADS, SB_HEAD_DIM))
            sbv_p.append(va_p.reshape(bp, sp, SB_HEADS, SB_HEAD_DIM))
            sbk_s.append(ka_s.reshape(bs, 1, SB_HEADS, SB_HEAD_DIM))
            sbv_s.append(va_s.reshape(bs, 1, SB_HEADS, SB_HEAD_DIM))
            ret_p.append(r_p)
            ret_s.append(r_s)
        else:
            w_in = ssd_w_in[i]
            ws = [w_in[:, :ssd_width].astype(BF16), w_in[:, ssd_width:ssd_width + cdim].astype(BF16),
                  pad_l(w_in[:, ssd_width + cdim:]).astype(BF16)]
            plan = ((F32,), (F32,), (F32,))
            z_p, xbc_p, dt_p = _inproj(xp, nw1, modp, "p", ws, plan, tps, "ssd_in_prompt")
            z_s, xbc_s, dt_s = _inproj(xs, nw1, mods, "s", ws, plan, 1, "ssd_in_sample")
            consts = (ssd_conv_w[i], row1(ssd_conv_b[i]), pad_l(row1(ssd_dt_bias[i])), pad_l(row1(ssd_a_log[i])),
                      row1(jnp.repeat(ssd_d[i], SSD_HEAD_DIM)), row1(ssd_norm_w[i]))
            mixed_p, s_p, cv_p = _ssd_prompt(z_p, xbc_p, dt_p, *consts, bp, sp, ssd_heads)
            mixed_s, s_s, cv_s = _ssd_sample(z_s, xbc_s, dt_s, state_conv, state_ssd, i, *consts, ssd_heads)
            parts_p, parts_s = [mixed_p], [mixed_s]
            w_parts = [ssd_w_out[i].astype(BF16)]
            ssd_p.append(s_p)
            ssd_s.append(s_s)
            conv_p.append(cv_p)
            conv_s.append(cv_s)

        rw = pad_l(router_w[layer]).astype(BF16)
        rb = jnp.pad(row1(router_b[layer]), ((0, 0), (0, LANES - n_exp)), constant_values=ROUTE_PAD_LOGIT)
        nw2 = row1(norm2_w[layer])
        cnt0 = jnp.zeros((1, LANES), F32)
        xp, h2p, rec_p, cnt_p = _post(parts_p, w_parts, xp, nw2, modp, "p", rw, rb, cnt0, tps, "post_prompt")
        xs, h2s, rec_s, cnt = _post(parts_s, w_parts, xs, nw2, mods, "s", rw, rb, cnt_p, 1, "post_sample")

        pad_start, block_expert, n_used = _expert_layout(cnt[0, :n_exp].astype(jnp.int32), n_blocks)
        pos_p = _row_positions(rec_p, pad_start)
        pos_s = _row_positions(rec_s, pad_start)
        xb = jnp.zeros((n_blocks * MOE_ROWS, d), F32)
        xb = _dispatch(pos_p, h2p, xb, "dispatch_prompt")
        xb = _dispatch(pos_s, h2s, xb, "dispatch_sample")
        yb = _experts(block_expert, n_used, xb, layer, moe_w1, moe_b1, moe_w2, moe_b2)
        last = layer == depth - 1
        fw = row1(final_norm_w) if last else None
        xp = _combine(pos_p, xp, modp, "p", rec_p, yb, tps, fw, "combine_prompt")
        xs = _combine(pos_s, xs, mods, "s", rec_s, yb, 1, fw, "combine_sample")

    st = lambda lst: jnp.stack(lst, axis=0)
    return (xp.reshape(bp, sp, d), xs.reshape(bs, ls, d), st(sbk_p), st(sbv_p), st(sbk_s), st(sbv_s),
            st(ret_p), st(ret_s), st(ssd_p), st(ssd_s), st(conv_p), st(conv_s))
```

```python
import functools
import math

import jax
import jax.numpy as jnp
from jax import lax
from jax.experimental import pallas as pl
from jax.experimental.pallas import tpu as pltpu

F32 = jnp.float32
BF16 = jnp.bfloat16

EPS = 1e-6
ROPE_BASE = 10000.0
SB_HEADS = 8
SB_HEAD_DIM = 64
SB_WIDTH = SB_HEADS * SB_HEAD_DIM
RET_HEADS = 4
RET_DIM = 128
RET_WIDTH = RET_HEADS * RET_DIM
SSD_HEAD_DIM = 64
SSD_GROUPS = 4
SSD_STATE = 128
SSD_CONV = 4
TOP_K = 4
SWIGLU_LIMIT = 7.0
SWIGLU_ALPHA = 1.702
CHUNK = 128
LANES = 128
ROW_TILE = 256
MOE_ROWS = 256
VMEM_LIMIT = 56 * 1024 * 1024


def _cparams(sem):
    return pltpu.CompilerParams(dimension_semantics=sem, vmem_limit_bytes=VMEM_LIMIT)


def _silu(x):
    return x * jax.nn.sigmoid(x)


def _softplus(x):
    return jnp.maximum(x, 0.0) + jnp.log1p(jnp.exp(-jnp.abs(x)))


def _modulate(x, nw, shift, scale):
    y = x * lax.rsqrt(jnp.mean(x * x, axis=-1, keepdims=True) + EPS)
    return (y * nw) * (1.0 + scale) + shift


def _split_bf16(x):
    hi = x.astype(BF16)
    lo = (x - hi.astype(F32)).astype(BF16)
    return hi, lo


def _dot(a, b):
    return jnp.dot(a, b, preferred_element_type=F32)


def _dot_nt(a, b):
    return lax.dot_general(a, b, (((1,), (1,)), ((), ())), preferred_element_type=F32)


def _dot_tn(a, b):
    return lax.dot_general(a, b, (((0,), (0,)), ((), ())), preferred_element_type=F32)


def _tri(n, strict, lower):
    r = lax.broadcasted_iota(jnp.int32, (n, n), 0)
    c = lax.broadcasted_iota(jnp.int32, (n, n), 1)
    if lower:
        m = (c < r) if strict else (c <= r)
    else:
        m = (c > r) if strict else (c >= r)
    return m


def _mod_specs(kind, ks, tm, tiles_per_seq, d):
    if kind == "p":
        return [pl.BlockSpec((None, 1, d), lambda i, k=k: ((i // tiles_per_seq) * 6 + k, 0, 0)) for k in ks]
    return [pl.BlockSpec((tm, d), lambda i, k=k: (i, k)) for k in ks]


def _ada_kernel(c_ref, w_ref, b_ref, o_ref):
    c = _silu(c_ref[...]).astype(BF16)
    o_ref[...] = _dot(c, w_ref[...].astype(BF16)) + b_ref[...]


def _ada_mod(c, w_ada, b_ada):
    depth, d, n = w_ada.shape
    rows = c.shape[0]
    tn = 1024
    return pl.pallas_call(
        _ada_kernel,
        grid=(depth, n // tn),
        in_specs=[pl.BlockSpec((rows, d), lambda l, j: (0, 0)),
                  pl.BlockSpec((None, d, tn), lambda l, j: (l, 0, j)),
                  pl.BlockSpec((None, 1, tn), lambda l, j: (l, 0, j))],
        out_specs=pl.BlockSpec((None, rows, tn), lambda l, j: (l, 0, j)),
        out_shape=jax.ShapeDtypeStruct((depth, rows, n), F32),
        compiler_params=_cparams(("arbitrary", "arbitrary")),
        name="ada_mod",
    )(c, w_ada, b_ada.reshape(depth, 1, n))


def _inproj_kernel(x_ref, nw_ref, sh_ref, sc_ref, *refs, plan):
    n_w = len(plan)
    w_refs, o_refs = refs[:n_w], refs[n_w:]
    h = _modulate(x_ref[...], nw_ref[...], sh_ref[...], sc_ref[...]).astype(BF16)
    oi = 0
    for w_ref, dts in zip(w_refs, plan):
        y = _dot(h, w_ref[...])
        for dt in dts:
            o_refs[oi][...] = y.astype(dt)
            oi += 1


def _inproj(x, nw, mod, kind, ws, plan, tiles_per_seq, name):
    t, d = x.shape
    tm = min(ROW_TILE, t)
    out_shape, out_specs = [], []
    for w, dts in zip(ws, plan):
        for dt in dts:
            out_shape.append(jax.ShapeDtypeStruct((t, w.shape[1]), dt))
            out_specs.append(pl.BlockSpec((tm, w.shape[1]), lambda i: (i, 0)))
    return pl.pallas_call(
        functools.partial(_inproj_kernel, plan=plan),
        grid=(t // tm,),
        in_specs=[pl.BlockSpec((tm, d), lambda i: (i, 0)),
                  pl.BlockSpec((1, d), lambda i: (0, 0))]
        + _mod_specs(kind, (0, 1), tm, tiles_per_seq, d)
        + [pl.BlockSpec(w.shape, lambda i: (0, 0)) for w in ws],
        out_specs=out_specs,
        out_shape=out_shape,
        compiler_params=_cparams(("arbitrary",)),
        name=name,
    )(x, nw, mod, mod, *ws)


ROUTE_GATE, ROUTE_IDX, ROUTE_RANK = 0, TOP_K, 2 * TOP_K
ROUTE_PAD_LOGIT = -1e30


def _route_tile(lg, cnt):
    tm = lg.shape[0]
    lane = lax.broadcasted_iota(jnp.int32, lg.shape, 1).astype(F32)
    work = lg
    vals, idxs = [], []
    for _ in range(TOP_K):
        m = jnp.max(work, axis=-1, keepdims=True)
        ix = jnp.min(jnp.where(work == m, lane, float(LANES)), axis=-1, keepdims=True)
        vals.append(m)
        idxs.append(ix)
        work = jnp.where(lane == ix, -jnp.inf, work)
    es = [jnp.exp(v - vals[0]) for v in vals]
    denom = es[0]
    for e in es[1:]:
        denom = denom + e
    onehot = jnp.zeros(lg.shape, F32)
    for ix in idxs:
        onehot = onehot + jnp.where(lane == ix, 1.0, 0.0)
    before = _tri(tm, strict=True, lower=True).astype(BF16)
    prefix = _dot(before, onehot.astype(BF16)) + cnt
    rec = jnp.zeros(lg.shape, F32)
    for j in range(TOP_K):
        rank = jnp.sum(jnp.where(lane == idxs[j], prefix, 0.0), axis=-1, keepdims=True)
        rec = jnp.where(lane == float(ROUTE_GATE + j), es[j] / denom, rec)
        rec = jnp.where(lane == float(ROUTE_IDX + j), idxs[j], rec)
        rec = jnp.where(lane == float(ROUTE_RANK + j), rank, rec)
    return rec, cnt + jnp.sum(onehot, axis=0, keepdims=True)


def _post_kernel(*refs, n_parts):
    m_refs, w_refs = refs[:n_parts], refs[n_parts:2 * n_parts]
    (x_ref, g_ref, nw_ref, sh_ref, sc_ref, rw_ref, rb_ref, cnt0_ref,
     xo_ref, h_ref, rt_ref, cnt_ref, cnt) = refs[2 * n_parts:]

    @pl.when(pl.program_id(0) == 0)
    def _():
        cnt[...] = cnt0_ref[...]

    o = _dot(m_refs[0][...], w_refs[0][...])
    for m_ref, w_ref in zip(m_refs[1:], w_refs[1:]):
        o = o + _dot(m_ref[...], w_ref[...])
    xn = x_ref[...] + g_ref[...] * o
    xo_ref[...] = xn
    h = _modulate(xn, nw_ref[...], sh_ref[...], sc_ref[...])
    h_ref[...] = h
    lg = _dot(h.astype(BF16), rw_ref[...]) + rb_ref[...]
    rec, cnt_new = _route_tile(lg, cnt[...])
    rt_ref[...] = rec
    cnt[...] = cnt_new
    cnt_ref[...] = cnt_new


def _post(parts, w_parts, x, nw2, mod, kind, rw, rb, cnt0, tiles_per_seq, name):
    t, d = x.shape
    tm = min(ROW_TILE, t)
    n_parts = len(parts)
    row = lambda n: pl.BlockSpec((tm, n), lambda i: (i, 0))
    full = lambda a: pl.BlockSpec(a.shape, lambda i: (0, 0))
    g_spec, sh_spec, sc_spec = _mod_specs(kind, (2, 3, 4), tm, tiles_per_seq, d)
    return pl.pallas_call(
        functools.partial(_post_kernel, n_parts=n_parts),
        grid=(t // tm,),
        in_specs=[row(p.shape[1]) for p in parts] + [full(w) for w in w_parts]
        + [row(d), g_spec, full(nw2), sh_spec, sc_spec, full(rw), full(rb), full(cnt0)],
        out_specs=[row(d), row(d), row(LANES), full(cnt0)],
        out_shape=[jax.ShapeDtypeStruct((t, d), F32), jax.ShapeDtypeStruct((t, d), F32),
                   jax.ShapeDtypeStruct((t, LANES), F32), jax.ShapeDtypeStruct(cnt0.shape, F32)],
        scratch_shapes=[pltpu.VMEM(cnt0.shape, F32)],
        compiler_params=_cparams(("arbitrary",)),
        name=name,
    )(*parts, *w_parts, x, mod, nw2, mod, mod, rw, rb, cnt0)


def _sb_prompt_kernel(bias_ref, q_ref, k_ref, v_ref, o_ref, acc_ref, run_ref):
    qi = pl.program_id(1)
    upper = _tri(CHUNK, strict=True, lower=True).astype(BF16)
    upper2 = jnp.concatenate([upper, upper], axis=0)
    causal = _tri(CHUNK, strict=True, lower=True)
    heads = range(SB_HEADS)
    cols = [slice(h * SB_HEAD_DIM, (h + 1) * SB_HEAD_DIM) for h in heads]

    def key_block(s0, diagonal):
        zs = [_dot_nt(q_ref[:, cols[h]], k_ref[pl.ds(s0, CHUNK), cols[h]]) for h in heads]
        ls, lks, splits = [], [], []
        for h in heads:
            z = zs[h] * (SB_HEAD_DIM ** -0.5) + bias_ref[h]
            sp = _softplus(z)
            lk = -sp
            if diagonal:
                lk = jnp.where(causal, lk, 0.0)
            ls.append(z - sp)
            lks.append(lk[:, 0:1])
            splits.append(jnp.concatenate(_split_bf16(lk), axis=1))
        betweens = [_dot(splits[h], upper2) for h in heads]
        ws = []
        for h in heads:
            between = betweens[h] if diagonal else betweens[h] + run_ref[h]
            w = jnp.exp(ls[h] + between)
            if diagonal:
                w = jnp.where(causal, w, 0.0)
            ws.append(w.astype(BF16))
            run_ref[h] = between[:, 0:1] + lks[h]
        outs = [_dot(ws[h], v_ref[pl.ds(s0, CHUNK), cols[h]]) for h in heads]
        for h in heads:
            if diagonal:
                acc_ref[:, cols[h]] = outs[h]
            else:
                acc_ref[:, cols[h]] += outs[h]

    key_block(pl.multiple_of(qi * CHUNK, CHUNK), True)

    def body(t, carry):
        key_block(pl.multiple_of((qi - t) * CHUNK, CHUNK), False)
        return carry

    lax.fori_loop(1, qi + 1, body, 0)
    o_ref[...] = acc_ref[...].astype(o_ref.dtype)


def _sb_prompt(q, k, v, bias, batch, seq):
    nq = seq // CHUNK
    return pl.pallas_call(
        _sb_prompt_kernel,
        grid=(batch, nq),
        in_specs=[pl.BlockSpec(memory_space=pltpu.SMEM),
                  pl.BlockSpec((CHUNK, SB_WIDTH), lambda b, i: (b * nq + i, 0)),
                  pl.BlockSpec((seq, SB_WIDTH), lambda b, i: (b, 0)),
                  pl.BlockSpec((seq, SB_WIDTH), lambda b, i: (b, 0))],
        out_specs=pl.BlockSpec((CHUNK, SB_WIDTH), lambda b, i: (b * nq + i, 0)),
        out_shape=jax.ShapeDtypeStruct((batch * seq, SB_WIDTH), BF16),
        scratch_shapes=[pltpu.VMEM((CHUNK, SB_WIDTH), F32), pltpu.VMEM((SB_HEADS, CHUNK, 1), F32)],
        compiler_params=_cparams(("arbitrary", "arbitrary")),
        name="sb_prompt",
    )(bias, q, k, v)


def _sb_sample_kernel(pt_ref, q_ref, bias_ref, *refs, n_pages):
    k_refs, v_refs, o_ref = refs[:n_pages], refs[n_pages:2 * n_pages], refs[2 * n_pages]
    row = lax.broadcasted_iota(jnp.int32, (SB_HEADS, SB_WIDTH), 0)
    col = lax.broadcasted_iota(jnp.int32, (SB_HEADS, SB_WIDTH), 1)
    own = (col // SB_HEAD_DIM) == row
    qm = jnp.where(own, jnp.broadcast_to(q_ref[...].astype(F32), (SB_HEADS, SB_WIDTH)), 0.0).astype(BF16)
    upper = _tri(CHUNK, strict=True, lower=True).astype(BF16)
    upper2 = jnp.concatenate([upper, upper], axis=0)
    bias = bias_ref[...]
    pages = range(n_pages)
    zs = [_dot_nt(qm, k_refs[p][...].astype(BF16)) * (SB_HEAD_DIM ** -0.5) + bias for p in pages]
    sps = [_softplus(z) for z in zs]
    splits = [jnp.concatenate(_split_bf16(-sp), axis=1) for sp in sps]
    betweens = [_dot(s, upper2) for s in splits]
    run = jnp.zeros((SB_HEADS, 1), F32)
    acc = jnp.zeros((SB_HEADS, SB_WIDTH), F32)
    for p in reversed(pages):
        between = betweens[p] + run
        w = jnp.exp((zs[p] - sps[p]) + between)
        acc = acc + _dot(w.astype(BF16), v_refs[p][...].astype(BF16))
        run = between[:, 0:1] - sps[p][:, 0:1]
    o_ref[...] = jnp.sum(jnp.where(own, acc, 0.0), axis=0, keepdims=True).astype(o_ref.dtype)


def _sb_sample(q, cache_k, cache_v, layer, page_table, bias):
    bs, n_pages = page_table.shape
    page = cache_k.shape[2]
    ck = cache_k.reshape(cache_k.shape[0], cache_k.shape[1], page, SB_WIDTH)
    cv = cache_v.reshape(cache_v.shape[0], cache_v.shape[1], page, SB_WIDTH)
    page_spec = lambda p: pl.BlockSpec((None, None, page, SB_WIDTH),
                                       lambda b, pt, p=p: (layer, pt[b * n_pages + p], 0, 0))
    grid_spec = pltpu.PrefetchScalarGridSpec(
        num_scalar_prefetch=1,
        grid=(bs,),
        in_specs=[pl.BlockSpec((None, 1, SB_WIDTH), lambda b, pt: (b, 0, 0)),
                  pl.BlockSpec((SB_HEADS, 1), lambda b, pt: (0, 0))]
        + [page_spec(p) for p in range(n_pages)] * 2,
        out_specs=pl.BlockSpec((None, 1, SB_WIDTH), lambda b, pt: (b, 0, 0)),
    )
    out = pl.pallas_call(
        functools.partial(_sb_sample_kernel, n_pages=n_pages),
        grid_spec=grid_spec,
        out_shape=jax.ShapeDtypeStruct((bs, 1, SB_WIDTH), BF16),
        compiler_params=_cparams(("arbitrary",)),
        name="sb_sample",
    )(page_table.reshape(-1), q.reshape(bs, 1, SB_WIDTH), bias.reshape(SB_HEADS, 1),
      *([ck] * n_pages), *([cv] * n_pages))
    return out.reshape(bs, SB_WIDTH)


def _log_gamma(h):
    return math.log1p(-(2.0 ** (-5.0 - h)))


def _rope_tables(pos):
    half = RET_DIM // 2
    inv = ROPE_BASE ** (-jnp.arange(half, dtype=F32) / half)
    ang = pos.astype(F32)[:, None] * inv[None, :]
    cos, sin = jnp.cos(ang), jnp.sin(ang)
    return jnp.concatenate([cos, cos], axis=-1), jnp.concatenate([-sin, sin], axis=-1)


def _rope(x, cos2, sin2):
    return x * cos2 + pltpu.roll(x, RET_DIM // 2, 1) * sin2


def _group_norm_gate(o, gn, g):
    mu = jnp.mean(o, axis=-1, keepdims=True)
    c = o - mu
    var = jnp.mean(c * c, axis=-1, keepdims=True)
    return _silu(g) * ((c * lax.rsqrt(var + EPS)) * gn)


def _ret_prompt_kernel(q_ref, k_ref, v_ref, g_ref, cos_ref, sin_ref, gn_ref, o_ref, r_ref, state):
    c = pl.program_id(1)

    @pl.when(c == 0)
    def _():
        state[...] = jnp.zeros_like(state)

    cos2, sin2 = cos_ref[...], sin_ref[...]
    li = lax.broadcasted_iota(jnp.int32, (CHUNK, CHUNK), 0)
    mi = lax.broadcasted_iota(jnp.int32, (CHUNK, CHUNK), 1)
    diff = li - mi
    idx = lax.broadcasted_iota(jnp.int32, (CHUNK, 1), 0).astype(F32)
    for h in range(RET_HEADS):
        lg = _log_gamma(h)
        cols = slice(h * RET_DIM, (h + 1) * RET_DIM)
        q = _rope(q_ref[:, cols], cos2, sin2)
        k = _rope(k_ref[:, cols], cos2, sin2) * (RET_DIM ** -0.5)
        v = v_ref[:, cols].astype(BF16)
        qb = q.astype(BF16)
        decay = jnp.where(diff >= 0, jnp.exp(jnp.maximum(diff, 0).astype(F32) * lg), 0.0)
        scores = _dot_nt(qb, k.astype(BF16)) * decay
        inner = _dot(scores.astype(BF16), v)
        r_old = state[h]
        from_state = _dot(qb, r_old.astype(BF16)) * jnp.exp((idx + 1.0) * lg)
        k_dec = k * jnp.exp((CHUNK - 1.0 - idx) * lg)
        state[h] = math.exp(CHUNK * lg) * r_old + _dot_tn(k_dec.astype(BF16), v)
        o = _group_norm_gate(inner + from_state, gn_ref[:, cols], g_ref[:, cols])
        o_ref[:, cols] = o.astype(o_ref.dtype)

    @pl.when(c == pl.num_programs(1) - 1)
    def _():
        r_ref[...] = state[...]


def _ret_prompt(ret, gn_w, batch, seq):
    nc = seq // CHUNK
    cos2, sin2 = _rope_tables(jnp.arange(seq, dtype=jnp.int32))
    col = lambda j: pl.BlockSpec((CHUNK, RET_WIDTH), lambda b, c, j=j: (b * nc + c, j))
    tab = pl.BlockSpec((CHUNK, RET_DIM), lambda b, c: (c, 0))
    return pl.pallas_call(
        _ret_prompt_kernel,
        grid=(batch, nc),
        in_specs=[col(0), col(1), col(2), col(3), tab, tab, pl.BlockSpec((1, RET_WIDTH), lambda b, c: (0, 0))],
        out_specs=[pl.BlockSpec((CHUNK, RET_WIDTH), lambda b, c: (b * nc + c, 0)),
                   pl.BlockSpec((None, RET_HEADS, RET_DIM, RET_DIM), lambda b, c: (b, 0, 0, 0))],
        out_shape=[jax.ShapeDtypeStruct((batch * seq, RET_WIDTH), BF16),
                   jax.ShapeDtypeStruct((batch, RET_HEADS, RET_DIM, RET_DIM), F32)],
        scratch_shapes=[pltpu.VMEM((RET_HEADS, RET_DIM, RET_DIM), F32)],
        compiler_params=_cparams(("arbitrary", "arbitrary")),
        name="ret_prompt",
    )(ret, ret, ret, ret, cos2, sin2, gn_w)


RET_SEQS = 8


def _ret_sample_kernel(q_ref, k_ref, v_ref, g_ref, cos_ref, sin_ref, gn_ref, r0_ref, o_ref, r_ref):
    cos2, sin2 = cos_ref[...], sin_ref[...]
    for h in range(RET_HEADS):
        gamma = math.exp(_log_gamma(h))
        cols = slice(h * RET_DIM, (h + 1) * RET_DIM)
        q = _rope(q_ref[:, cols], cos2, sin2)
        k = _rope(k_ref[:, cols], cos2, sin2) * (RET_DIM ** -0.5)
        v = v_ref[:, cols]
        qb, kb, vb = q.astype(BF16), k.astype(BF16), v.astype(BF16)
        qk = jnp.sum(qb.astype(F32) * kb.astype(F32), axis=-1, keepdims=True)
        inner = qk * vb.astype(F32)
        rows = []
        for s in range(RET_SEQS):
            r_old = r0_ref[s, h]
            rows.append(_dot(qb[s:s + 1, :], r_old.astype(BF16)) * gamma)
            k_col = jnp.transpose(kb[s:s + 1, :].astype(F32))
            r_ref[s, h] = gamma * r_old + k_col * vb[s:s + 1, :].astype(F32)
        o = inner + jnp.concatenate(rows, axis=0)
        o_ref[:, cols] = _group_norm_gate(o, gn_ref[:, cols], g_ref[:, cols]).astype(o_ref.dtype)


def _ret_sample(ret, gn_w, r0, layer, past_len):
    bs = ret.shape[0]
    cos2, sin2 = _rope_tables(jnp.full((1,), past_len, jnp.int32))
    col = lambda j: pl.BlockSpec((RET_SEQS, RET_WIDTH), lambda i, j=j: (i, j))
    one = lambda n: pl.BlockSpec((1, n), lambda i: (0, 0))
    st_in = pl.BlockSpec((None, RET_SEQS, RET_HEADS, RET_DIM, RET_DIM), lambda i: (layer, i, 0, 0, 0))
    st = pl.BlockSpec((RET_SEQS, RET_HEADS, RET_DIM, RET_DIM), lambda i: (i, 0, 0, 0))
    return pl.pallas_call(
        _ret_sample_kernel,
        grid=(bs // RET_SEQS,),
        in_specs=[col(0), col(1), col(2), col(3), one(RET_DIM), one(RET_DIM), one(RET_WIDTH), st_in],
        out_specs=[pl.BlockSpec((RET_SEQS, RET_WIDTH), lambda i: (i, 0)), st],
        out_shape=[jax.ShapeDtypeStruct((bs, RET_WIDTH), BF16),
                   jax.ShapeDtypeStruct(r0.shape[1:], F32)],
        compiler_params=_cparams(("arbitrary",)),
        name="ret_sample",
    )(ret, ret, ret, ret, cos2, sin2, gn_w, r0)


def _gated_group_rms(y, z, nw, groups):
    yg = y * _silu(z)
    gw = y.shape[-1] // groups
    outs = []
    for g in range(groups):
        a = yg[:, g * gw:(g + 1) * gw]
        outs.append(a * lax.rsqrt(jnp.mean(a * a, axis=-1, keepdims=True) + EPS))
    return jnp.concatenate(outs, axis=-1) * nw


def _ssd_prompt_kernel(z_ref, xbc_ref, dt_ref, cw_ref, cb_ref, dtb_ref, alog_ref, dsk_ref, nw_ref, sel_ref,
                       y_ref, s_ref, cv_ref, state, win, ybuf, *, heads, width):
    c = pl.program_id(1)
    hpg = heads // SSD_GROUPS
    gn = SSD_GROUPS * SSD_STATE

    @pl.when(c == 0)
    def _():
        state[...] = jnp.zeros_like(state)
        win[0:8, :] = jnp.zeros((8, win.shape[1]), F32)

    win[8:8 + CHUNK, :] = xbc_ref[...]
    conv = cb_ref[...]
    for j in range(SSD_CONV):
        conv = conv + win[5 + j:5 + j + CHUNK, :] * cw_ref[j:j + 1, :]
    act = _silu(conv)

    @pl.when(c == pl.num_programs(1) - 1)
    def _():
        cv_ref[...] = win[CHUNK + 5:CHUNK + 8, :]

    win[0:8, :] = win[CHUNK:CHUNK + 8, :]

    dt = _softplus(dt_ref[...] + dtb_ref[...])
    a = -jnp.exp(alog_ref[...])
    da_hi, da_lo = _split_bf16(dt * a)
    lower = _tri(CHUNK, strict=False, lower=True).astype(BF16)
    acs = _dot(lower, da_hi) + _dot(lower, da_lo)
    acs_t = jnp.transpose(acs)
    dt_t = jnp.transpose(dt)
    both = jnp.concatenate([acs, dt], axis=0)
    p0 = both.astype(BF16)
    r0 = both - p0.astype(F32)
    p1 = r0.astype(BF16)
    p2 = (r0 - p1.astype(F32)).astype(BF16)
    wide = _dot(jnp.concatenate([p0, p1, p2], axis=1), sel_ref[...])
    causal = _tri(CHUNK, strict=False, lower=True)
    p = SSD_HEAD_DIM
    for g in range(SSD_GROUPS):
        bg = act[:, width + g * SSD_STATE:width + (g + 1) * SSD_STATE].astype(BF16)
        cg = act[:, width + gn + g * SSD_STATE:width + gn + (g + 1) * SSD_STATE].astype(BF16)
        cbm = _dot_nt(cg, bg)
        for r in range(hpg):
            h = g * hpg + r
            cols = slice(h * p, (h + 1) * p)
            xh = act[:, cols]
            acs_b = wide[:CHUNK, h * LANES:(h + 1) * LANES]
            dt_b = wide[CHUNK:, h * LANES:h * LANES + p]
            seg = jnp.where(causal, acs_b - acs_t[h:h + 1, :], -jnp.inf)
            w = cbm * jnp.exp(seg) * dt_t[h:h + 1, :]
            s_old = state[h]
            y = _dot(w.astype(BF16), xh.astype(BF16))
            y = y + _dot_nt(cg, s_old.astype(BF16)) * jnp.exp(acs_b[:, :p])
            last = acs_b[CHUNK - 1:CHUNK, :]
            to_end = jnp.exp(last[:, :p] - acs_b[:, :p]) * dt_b
            state[h] = jnp.exp(last) * s_old + _dot_tn((xh * to_end).astype(BF16), bg)
            ybuf[:, cols] = y + dsk_ref[:, cols] * xh

    y_ref[...] = _gated_group_rms(ybuf[...], z_ref[...], nw_ref[...], SSD_GROUPS).astype(y_ref.dtype)

    @pl.when(c == pl.num_programs(1) - 1)
    def _():
        s_ref[...] = state[...]


def _ssd_prompt(z, xbc, dt, cw, cb, dtb, alog, dsk, nw, batch, seq, heads):
    nc = seq // CHUNK
    width = z.shape[1]
    cdim = xbc.shape[1]
    row = lambda n: pl.BlockSpec((CHUNK, n), lambda b, c: (b * nc + c, 0))
    full = lambda a: pl.BlockSpec(a.shape, lambda b, c: (0, 0))
    lane = jnp.arange(LANES, dtype=jnp.int32)[:, None]
    tile = jnp.arange(heads * LANES, dtype=jnp.int32)[None, :] // LANES
    sel = jnp.tile((lane == tile).astype(BF16), (3, 1))
    return pl.pallas_call(
        functools.partial(_ssd_prompt_kernel, heads=heads, width=width),
        grid=(batch, nc),
        in_specs=[row(width), row(cdim), row(LANES), full(cw), full(cb), full(dtb), full(alog), full(dsk), full(nw),
                  full(sel)],
        out_specs=[row(width),
                   pl.BlockSpec((None, heads, SSD_HEAD_DIM, SSD_STATE), lambda b, c: (b, 0, 0, 0)),
                   pl.BlockSpec((None, SSD_CONV - 1, cdim), lambda b, c: (b, 0, 0))],
        out_shape=[jax.ShapeDtypeStruct((batch * seq, width), BF16),
                   jax.ShapeDtypeStruct((batch, heads, SSD_HEAD_DIM, SSD_STATE), F32),
                   jax.ShapeDtypeStruct((batch, SSD_CONV - 1, cdim), F32)],
        scratch_shapes=[pltpu.VMEM((heads, SSD_HEAD_DIM, SSD_STATE), F32),
                        pltpu.VMEM((CHUNK + 8, cdim), F32),
                        pltpu.VMEM((CHUNK, width), F32)],
        compiler_params=_cparams(("arbitrary", "arbitrary")),
        name="ssd_prompt",
    )(z, xbc, dt, cw, cb, dtb, alog, dsk, nw, sel)


SSD_SEQS = 4


def _ssd_sample_kernel(z_ref, xbc_ref, dt_ref, cprev_ref, cw_ref, cb_ref, dtb_ref, alog_ref, dsk_ref, nw_ref,
                       s0_ref, y_ref, s_ref, cv_ref, *, heads, width):
    hpg = heads // SSD_GROUPS
    gn = SSD_GROUPS * SSD_STATE
    for s in range(SSD_SEQS):
        dt = _softplus(dt_ref[s] + dtb_ref[...])
        decay = jnp.exp(dt * (-jnp.exp(alog_ref[...])))
        xrow = xbc_ref[s]
        prev = cprev_ref[s]
        conv = cb_ref[...] + xrow * cw_ref[SSD_CONV - 1:SSD_CONV, :]
        for j in range(SSD_CONV - 1):
            conv = conv + prev[j:j + 1, :] * cw_ref[j:j + 1, :]
        cv_ref[s, 0:SSD_CONV - 2, :] = prev[1:SSD_CONV - 1, :]
        cv_ref[s, SSD_CONV - 2:SSD_CONV - 1, :] = xrow
        act = _silu(conv)
        yrow = []
        for g in range(SSD_GROUPS):
            bg = act[:, width + g * SSD_STATE:width + (g + 1) * SSD_STATE].astype(BF16).astype(F32)
            cg = act[:, width + gn + g * SSD_STATE:width + gn + (g + 1) * SSD_STATE].astype(BF16)
            for r in range(hpg):
                h = g * hpg + r
                cols = slice(h * SSD_HEAD_DIM, (h + 1) * SSD_HEAD_DIM)
                xh = act[:, cols]
                dtx = (xh * dt[:, h:h + 1]).astype(BF16).astype(F32)
                s_new = decay[:, h:h + 1] * s0_ref[s, h] + jnp.transpose(dtx) * bg
                s_ref[s, h] = s_new
                yrow.append(_dot_nt(cg, s_new.astype(BF16)) + dsk_ref[:, cols] * xh)
        y = jnp.concatenate(yrow, axis=-1)
        y_ref[s] = _gated_group_rms(y, z_ref[s], nw_ref[...], SSD_GROUPS).astype(y_ref.dtype)


def _ssd_sample(z, xbc, dt, conv_prev, s0, layer, cw, cb, dtb, alog, dsk, nw, heads):
    bs, width = z.shape
    cdim = xbc.shape[1]
    row = lambda n: pl.BlockSpec((SSD_SEQS, 1, n), lambda i: (i, 0, 0))
    full = lambda a: pl.BlockSpec(a.shape, lambda i: (0, 0))
    st_in = pl.BlockSpec((None, SSD_SEQS, heads, SSD_HEAD_DIM, SSD_STATE), lambda i: (layer, i, 0, 0, 0))
    cv_in = pl.BlockSpec((None, SSD_SEQS, SSD_CONV - 1, cdim), lambda i: (layer, i, 0, 0))
    st = pl.BlockSpec((SSD_SEQS, heads, SSD_HEAD_DIM, SSD_STATE), lambda i: (i, 0, 0, 0))
    cv = pl.BlockSpec((SSD_SEQS, SSD_CONV - 1, cdim), lambda i: (i, 0, 0))
    y, s_new, cv_new = pl.pallas_call(
        functools.partial(_ssd_sample_kernel, heads=heads, width=width),
        grid=(bs // SSD_SEQS,),
        in_specs=[row(width), row(cdim), row(LANES), cv_in, full(cw), full(cb), full(dtb), full(alog), full(dsk),
                  full(nw), st_in],
        out_specs=[row(width), st, cv],
        out_shape=[jax.ShapeDtypeStruct((bs, 1, width), BF16),
                   jax.ShapeDtypeStruct(s0.shape[1:], F32),
                   jax.ShapeDtypeStruct(conv_prev.shape[1:], F32)],
        compiler_params=_cparams(("arbitrary",)),
        name="ssd_sample",
    )(z.reshape(bs, 1, width), xbc.reshape(bs, 1, cdim), dt.reshape(bs, 1, LANES), conv_prev, cw, cb, dtb, alog,
      dsk, nw, s0)
    return y.reshape(bs, width), s_new, cv_new


def _expert_kernel(be_ref, nu_ref, x_ref, w1_ref, b1_ref, w2_ref, b2_ref, o_ref, w1s, w2s, *, f):
    i = pl.program_id(0)
    prev = be_ref[jnp.maximum(i - 1, 0)]

    @pl.when((i == 0) | (be_ref[i] != prev))
    def _():
        w1s[...] = w1_ref[...].astype(BF16)
        w2s[...] = w2_ref[...].astype(BF16)

    @pl.when(i < nu_ref[0])
    def _():
        x = x_ref[...].astype(BF16)
        acc = jnp.zeros(o_ref.shape, F32) + b2_ref[...]
        half = f // 2
        for c in range(2):
            glu = _dot(x, w1s[:, c * half:(c + 1) * half]) + b1_ref[:, c * half:(c + 1) * half]
            lin = _dot(x, w1s[:, f + c * half:f + (c + 1) * half]) + b1_ref[:, f + c * half:f + (c + 1) * half]
            glu = jnp.minimum(glu, SWIGLU_LIMIT)
            lin = jnp.clip(lin, -SWIGLU_LIMIT, SWIGLU_LIMIT)
            a = glu * jax.nn.sigmoid(SWIGLU_ALPHA * glu) * (lin + 1.0)
            acc = acc + _dot(a.astype(BF16), w2s[c * half:(c + 1) * half, :])
        o_ref[...] = acc

    @pl.when(i >= nu_ref[0])
    def _():
        o_ref[...] = jnp.zeros_like(o_ref)


def _experts(block_expert, n_used, xb, layer, w1, b1, w2, b2):
    n_rows = xb.shape[0]
    _, n_exp, d, f2 = w1.shape
    f = f2 // 2
    n_blocks = n_rows // MOE_ROWS
    rows = pl.BlockSpec((MOE_ROWS, d), lambda i, be, nu: (i, 0))
    grid_spec = pltpu.PrefetchScalarGridSpec(
        num_scalar_prefetch=2,
        grid=(n_blocks,),
        in_specs=[rows,
                  pl.BlockSpec((None, None, d, f2), lambda i, be, nu: (layer, be[i], 0, 0)),
                  pl.BlockSpec((None, None, 1, f2), lambda i, be, nu: (layer, be[i], 0, 0)),
                  pl.BlockSpec((None, None, f, d), lambda i, be, nu: (layer, be[i], 0, 0)),
                  pl.BlockSpec((None, None, 1, d), lambda i, be, nu: (layer, be[i], 0, 0))],
        out_specs=rows,
        scratch_shapes=[pltpu.VMEM((d, f2), BF16), pltpu.VMEM((f, d), BF16)],
    )
    depth = w1.shape[0]
    return pl.pallas_call(
        functools.partial(_expert_kernel, f=f),
        grid_spec=grid_spec,
        out_shape=jax.ShapeDtypeStruct(xb.shape, F32),
        compiler_params=_cparams(("arbitrary",)),
        name="moe_experts",
    )(block_expert, n_used, xb, w1, b1.reshape(depth, n_exp, 1, f2), w2, b2.reshape(depth, n_exp, 1, d))


def _expert_layout(counts, n_blocks):
    n_exp = counts.shape[0]
    padded = (counts + MOE_ROWS - 1) // MOE_ROWS * MOE_ROWS
    pad_end = jnp.cumsum(padded)
    starts = jnp.arange(n_blocks, dtype=jnp.int32) * MOE_ROWS
    block_expert = jnp.minimum(jnp.sum(starts[:, None] >= pad_end[None, :], axis=1), n_exp - 1).astype(jnp.int32)
    return pad_end - padded, block_expert, (pad_end[-1] // MOE_ROWS).astype(jnp.int32).reshape(1)


def _row_positions(rec, pad_start):
    n_exp = pad_start.shape[0]
    idx = rec[:, ROUTE_IDX:ROUTE_IDX + TOP_K].astype(jnp.int32)
    rank = rec[:, ROUTE_RANK:ROUTE_RANK + TOP_K].astype(jnp.int32)
    start = jnp.sum(jnp.where(idx[..., None] == jnp.arange(n_exp, dtype=jnp.int32), pad_start, 0), axis=-1)
    return (start + rank).reshape(-1).astype(jnp.int32)


SUBLANES = 8
COMBINE_ROWS = 32


def _dispatch_kernel(pos_ref, h_ref, xb_in, xb_ref, sem):
    del xb_in
    tm = h_ref.shape[0]
    base = pl.program_id(0) * (tm * TOP_K)

    def row_copy(g, k, j):
        t0 = pl.multiple_of(g * SUBLANES, SUBLANES)
        dst = pos_ref[base + g * (SUBLANES * TOP_K) + (k * TOP_K + j)]
        return pltpu.make_async_copy(h_ref.at[pl.ds(t0 + k, 1), :], xb_ref.at[pl.ds(dst, 1), :], sem)

    def issue(g, carry):
        for k in range(SUBLANES):
            for j in range(TOP_K):
                row_copy(g, k, j).start()
        return carry

    def drain(g, carry):
        for k in range(SUBLANES):
            for j in range(TOP_K):
                row_copy(g, k, j).wait()
        return carry

    lax.fori_loop(0, tm // SUBLANES, issue, 0)
    lax.fori_loop(0, tm // SUBLANES, drain, 0)


def _dispatch(pos, h, xb, name):
    t, d = h.shape
    tm = min(ROW_TILE, t)
    grid_spec = pltpu.PrefetchScalarGridSpec(
        num_scalar_prefetch=1,
        grid=(t // tm,),
        in_specs=[pl.BlockSpec((tm, d), lambda i, p: (i, 0)),
                  pl.BlockSpec(memory_space=pl.ANY)],
        out_specs=pl.BlockSpec(memory_space=pl.ANY),
        scratch_shapes=[pltpu.SemaphoreType.DMA],
    )
    return pl.pallas_call(
        _dispatch_kernel,
        grid_spec=grid_spec,
        out_shape=jax.ShapeDtypeStruct(xb.shape, xb.dtype),
        input_output_aliases={2: 0},
        compiler_params=_cparams(("arbitrary",)),
        name=name,
    )(pos, h, xb)


def _combine_kernel(pos_ref, x_ref, g_ref, rec_ref, yb_ref, *rest, final):
    if final:
        fw_ref, o_ref, ybuf, sem = rest
    else:
        o_ref, ybuf, sem = rest
    tm = x_ref.shape[0]
    base = pl.program_id(0) * (tm * TOP_K)

    def row_copy(g, k, j):
        t0 = pl.multiple_of(g * SUBLANES, SUBLANES)
        src = pos_ref[base + g * (SUBLANES * TOP_K) + (k * TOP_K + j)]
        return pltpu.make_async_copy(yb_ref.at[pl.ds(src, 1), :], ybuf.at[j, pl.ds(t0 + k, 1), :], sem)

    def issue(g, carry):
        for k in range(SUBLANES):
            for j in range(TOP_K):
                row_copy(g, k, j).start()
        return carry

    def drain(g, carry):
        for k in range(SUBLANES):
            for j in range(TOP_K):
                row_copy(g, k, j).wait()
        return carry

    lax.fori_loop(0, tm // SUBLANES, issue, 0)
    lax.fori_loop(0, tm // SUBLANES, drain, 0)

    def rows(c, carry):
        r = pl.ds(pl.multiple_of(c * COMBINE_ROWS, COMBINE_ROWS), COMBINE_ROWS)
        rec = rec_ref[r, :]
        y = rec[:, ROUTE_GATE:ROUTE_GATE + 1] * ybuf[0, r, :]
        for j in range(1, TOP_K):
            y = y + rec[:, ROUTE_GATE + j:ROUTE_GATE + j + 1] * ybuf[j, r, :]
        g = g_ref[...] if g_ref.shape[0] == 1 else g_ref[r, :]
        xn = x_ref[r, :] + g * y
        if final:
            xn = (xn * lax.rsqrt(jnp.mean(xn * xn, axis=-1, keepdims=True) + EPS)) * fw_ref[...]
        o_ref[r, :] = xn
        return carry

    lax.fori_loop(0, tm // COMBINE_ROWS, rows, 0)


def _combine(pos, x, mod, kind, rec, yb, tiles_per_seq, final_w, name):
    t, d = x.shape
    tm = min(ROW_TILE, t)
    (g_spec,) = _mod_specs(kind, (5,), tm, tiles_per_seq, d)
    with_p = lambda spec: pl.BlockSpec(spec.block_shape, lambda i, p, f=spec.index_map: f(i))
    in_specs = [pl.BlockSpec((tm, d), lambda i, p: (i, 0)), with_p(g_spec),
                pl.BlockSpec((tm, LANES), lambda i, p: (i, 0)),
                pl.BlockSpec(memory_space=pl.ANY)]
    args = [pos, x, mod, rec, yb]
    if final_w is not None:
        in_specs.append(pl.BlockSpec((1, d), lambda i, p: (0, 0)))
        args.append(final_w)
    grid_spec = pltpu.PrefetchScalarGridSpec(
        num_scalar_prefetch=1,
        grid=(t // tm,),
        in_specs=in_specs,
        out_specs=pl.BlockSpec((tm, d), lambda i, p: (i, 0)),
        scratch_shapes=[pltpu.VMEM((TOP_K, tm, d), F32), pltpu.SemaphoreType.DMA],
    )
    return pl.pallas_call(
        functools.partial(_combine_kernel, final=final_w is not None),
        grid_spec=grid_spec,
        out_shape=jax.ShapeDtypeStruct((t, d), F32),
        compiler_params=_cparams(("arbitrary",)),
        name=name,
    )(*args)


def kernel(x_prompt, x_sample, cache_sb_k, cache_sb_v, page_table, state_ret, state_ssd, state_conv,
           c_prompt, c_sample, norm1_w, norm2_w, w_ada, b_ada, ab_w_in, ab_w_out, ret_gn_w, sb_bias,
           ssd_w_in, ssd_conv_w, ssd_conv_b, ssd_dt_bias, ssd_a_log, ssd_d, ssd_norm_w, ssd_w_out,
           router_w, router_b, moe_w1, moe_b1, moe_w2, moe_b2, final_norm_w):
    bp, sp, d = x_prompt.shape
    bs, ls, _ = x_sample.shape
    assert ls == 1 and sp % CHUNK == 0 and cache_sb_k.shape[2] == CHUNK
    depth = w_ada.shape[0]
    n_exp = router_w.shape[-1]
    tp = bp * sp
    past_len = page_table.shape[1] * cache_sb_k.shape[2]
    ssd_width = ssd_norm_w.shape[-1]
    ssd_heads = ssd_a_log.shape[-1]
    cdim = ssd_conv_w.shape[-1]
    tps = sp // min(ROW_TILE, tp)
    assert d % LANES == 0 and n_exp <= LANES
    n_blocks = -(-((tp + bs) * TOP_K) // MOE_ROWS) + n_exp

    mod = _ada_mod(jnp.concatenate([c_prompt, c_sample], axis=0), w_ada, b_ada)
    xp = x_prompt.reshape(tp, d)
    xs = x_sample.reshape(bs, d)
    row1 = lambda a: a.reshape(1, -1)
    pad_l = lambda a: jnp.pad(a, ((0, 0), (0, LANES - a.shape[-1])))

    sbk_p, sbv_p, sbk_s, sbv_s, ret_p, ret_s = [], [], [], [], [], []
    ssd_p, ssd_s, conv_p, conv_s = [], [], [], []
    y_prompt = y_sample = None
    xb = jnp.zeros((n_blocks * MOE_ROWS, d), F32)
    for layer in range(depth):
        i = layer // 2
        modp = mod[layer, :bp].reshape(bp * 6, 1, d)
        mods = mod[layer, bp:]
        nw1 = row1(norm1_w[layer])
        if layer % 2 == 0:
            w_in = ab_w_in[i].astype(BF16)
            ws = [w_in[:, :SB_WIDTH], w_in[:, SB_WIDTH:2 * SB_WIDTH], w_in[:, 2 * SB_WIDTH:3 * SB_WIDTH],
                  w_in[:, 3 * SB_WIDTH:]]
            plan = ((BF16,), (F32, BF16), (F32, BF16), (F32,))
            qa_p, ka_p, kab_p, va_p, vab_p, rt_p = _inproj(xp, nw1, modp, "p", ws, plan, tps, "ab_in_prompt")
            qa_s, ka_s, _, va_s, _, rt_s = _inproj(xs, nw1, mods, "s", ws, plan, 1, "ab_in_sample")
            gn = row1(ret_gn_w[i])
            o_sb_p = _sb_prompt(qa_p, kab_p, vab_p, sb_bias[i], bp, sp)
            o_rt_p, r_p = _ret_prompt(rt_p, gn, bp, sp)
            o_sb_s = _sb_sample(qa_s, cache_sb_k, cache_sb_v, i, page_table, sb_bias[i])
            o_rt_s, r_s = _ret_sample(rt_s, gn, state_ret, i, past_len)
            parts_p, parts_s = [o_sb_p, o_rt_p], [o_sb_s, o_rt_s]
            w_out = ab_w_out[i].astype(BF16)
            w_parts = [w_out[:SB_WIDTH], w_out[SB_WIDTH:]]
            sbk_p.append(ka_p.reshape(bp, sp, SB_HEADS, SB_HEAD_DIM))
            sbv_p.append(va_p.reshape(bp, sp, SB_HEADS, SB_HEAD_DIM))
            sbk_s.append(ka_s.reshape(bs, 1, SB_HEADS, SB_HEAD_DIM))
            sbv_s.append(va_s.reshape(bs, 1, SB_HEADS, SB_HEAD_DIM))
            ret_p.append(r_p)
            ret_s.append(r_s)
        else:
            w_in = ssd_w_in[i]
            ws = [w_in[:, :ssd_width].astype(BF16), w_in[:, ssd_width:ssd_width + cdim].astype(BF16),
                  pad_l(w_in[:, ssd_width + cdim:]).astype(BF16)]
            plan = ((F32,), (F32,), (F32,))
            z_p, xbc_p, dt_p = _inproj(xp, nw1, modp, "p", ws, plan, tps, "ssd_in_prompt")
            z_s, xbc_s, dt_s = _inproj(xs, nw1, mods, "s", ws, plan, 1, "ssd_in_sample")
            consts = (ssd_conv_w[i], row1(ssd_conv_b[i]), pad_l(row1(ssd_dt_bias[i])), pad_l(row1(ssd_a_log[i])),
                      row1(jnp.repeat(ssd_d[i], SSD_HEAD_DIM)), row1(ssd_norm_w[i]))
            mixed_p, s_p, cv_p = _ssd_prompt(z_p, xbc_p, dt_p, *consts, bp, sp, ssd_heads)
            mixed_s, s_s, cv_s = _ssd_sample(z_s, xbc_s, dt_s, state_conv, state_ssd, i, *consts, ssd_heads)
            parts_p, parts_s = [mixed_p], [mixed_s]
            w_parts = [ssd_w_out[i].astype(BF16)]
            ssd_p.append(s_p)
            ssd_s.append(s_s)
            conv_p.append(cv_p)
            conv_s.append(cv_s)

        rw = pad_l(router_w[layer]).astype(BF16)
        rb = jnp.pad(row1(router_b[layer]), ((0, 0), (0, LANES - n_exp)), constant_values=ROUTE_PAD_LOGIT)
        nw2 = row1(norm2_w[layer])
        cnt0 = jnp.zeros((1, LANES), F32)
        xp, h2p, rec_p, cnt_p = _post(parts_p, w_parts, xp, nw2, modp, "p", rw, rb, cnt0, tps, "post_prompt")
        xs, h2s, rec_s, cnt = _post(parts_s, w_parts, xs, nw2, mods, "s", rw, rb, cnt_p, 1, "post_sample")

        pad_start, block_expert, n_used = _expert_layout(cnt[0, :n_exp].astype(jnp.int32), n_blocks)
        pos_p = _row_positions(rec_p, pad_start)
        pos_s = _row_positions(rec_s, pad_start)
        xb = _dispatch(pos_p, h2p, xb, "dispatch_prompt")
        xb = _dispatch(pos_s, h2s, xb, "dispatch_sample")
        yb = _experts(block_expert, n_used, xb, layer, moe_w1, moe_b1, moe_w2, moe_b2)
        last = layer == depth - 1
        fw = row1(final_norm_w) if last else None
        xp = _combine(pos_p, xp, modp, "p", rec_p, yb, tps, fw, "combine_prompt")
        xs = _combine(pos_s, xs, mods, "s", rec_s, yb, 1, fw, "combine_sample")

    st = lambda lst: jnp.stack(lst, axis=0)
    return (xp.reshape(bp, sp, d), xs.reshape(bs, ls, d), st(sbk_p), st(sbv_p), st(sbk_s), st(sbv_s),
            st(ret_p), st(ret_s), st(ssd_p), st(ssd_s), st(conv_p), st(conv_s))
```

```python
import functools
import math

import jax
import jax.numpy as jnp
from jax import lax
from jax.experimental import pallas as pl
from jax.experimental.pallas import tpu as pltpu

F32 = jnp.float32
BF16 = jnp.bfloat16

EPS = 1e-6
ROPE_BASE = 10000.0
SB_HEADS = 8
SB_HEAD_DIM = 64
SB_WIDTH = SB_HEADS * SB_HEAD_DIM
RET_HEADS = 4
RET_DIM = 128
RET_WIDTH = RET_HEADS * RET_DIM
SSD_HEAD_DIM = 64
SSD_GROUPS = 4
SSD_STATE = 128
SSD_CONV = 4
TOP_K = 4
SWIGLU_LIMIT = 7.0
SWIGLU_ALPHA = 1.702
CHUNK = 128
LANES = 128
ROW_TILE = 256
MOE_ROWS = 256
VMEM_LIMIT = 56 * 1024 * 1024


def _cparams(sem):
    return pltpu.CompilerParams(dimension_semantics=sem, vmem_limit_bytes=VMEM_LIMIT)


def _silu(x):
    return x * jax.nn.sigmoid(x)


def _softplus(x):
    return jnp.maximum(x, 0.0) + jnp.log1p(jnp.exp(-jnp.abs(x)))


def _softplus_sum(x):
    return jnp.maximum(x, 0.0) + jnp.log(1.0 + jnp.exp(-jnp.abs(x)))


def _modulate(x, nw, shift, scale):
    y = x * lax.rsqrt(jnp.mean(x * x, axis=-1, keepdims=True) + EPS)
    return (y * nw) * (1.0 + scale) + shift


def _split_bf16(x):
    hi = x.astype(BF16)
    lo = (x - hi.astype(F32)).astype(BF16)
    return hi, lo


def _dot(a, b):
    return jnp.dot(a, b, preferred_element_type=F32)


def _dot_nt(a, b):
    return lax.dot_general(a, b, (((1,), (1,)), ((), ())), preferred_element_type=F32)


def _dot_tn(a, b):
    return lax.dot_general(a, b, (((0,), (0,)), ((), ())), preferred_element_type=F32)


def _tri(n, strict, lower):
    r = lax.broadcasted_iota(jnp.int32, (n, n), 0)
    c = lax.broadcasted_iota(jnp.int32, (n, n), 1)
    if lower:
        m = (c < r) if strict else (c <= r)
    else:
        m = (c > r) if strict else (c >= r)
    return m


def _mod_specs(kind, ks, tm, tiles_per_seq, d):
    if kind == "p":
        return [pl.BlockSpec((None, 1, d), lambda i, k=k: ((i // tiles_per_seq) * 6 + k, 0, 0)) for k in ks]
    return [pl.BlockSpec((tm, d), lambda i, k=k: (i, k)) for k in ks]


def _ada_kernel(c_ref, w_ref, b_ref, o_ref):
    c = _silu(c_ref[...]).astype(BF16)
    o_ref[...] = _dot(c, w_ref[...].astype(BF16)) + b_ref[...]


def _ada_mod(c, w_ada, b_ada):
    depth, d, n = w_ada.shape
    rows = c.shape[0]
    tn = 1024
    return pl.pallas_call(
        _ada_kernel,
        grid=(depth, n // tn),
        in_specs=[pl.BlockSpec((rows, d), lambda l, j: (0, 0)),
                  pl.BlockSpec((None, d, tn), lambda l, j: (l, 0, j)),
                  pl.BlockSpec((None, 1, tn), lambda l, j: (l, 0, j))],
        out_specs=pl.BlockSpec((None, rows, tn), lambda l, j: (l, 0, j)),
        out_shape=jax.ShapeDtypeStruct((depth, rows, n), F32),
        compiler_params=_cparams(("arbitrary", "arbitrary")),
        name="ada_mod",
    )(c, w_ada, b_ada.reshape(depth, 1, n))


def _inproj_kernel(x_ref, nw_ref, sh_ref, sc_ref, *refs, plan):
    n_w = len(plan)
    w_refs, o_refs = refs[:n_w], refs[n_w:]
    h = _modulate(x_ref[...], nw_ref[...], sh_ref[...], sc_ref[...]).astype(BF16)
    oi = 0
    for w_ref, dts in zip(w_refs, plan):
        y = _dot(h, w_ref[...])
        for dt in dts:
            o_refs[oi][...] = y.astype(dt)
            oi += 1


def _inproj(x, nw, mod, kind, ws, plan, tiles_per_seq, name):
    t, d = x.shape
    tm = min(ROW_TILE, t)
    out_shape, out_specs = [], []
    for w, dts in zip(ws, plan):
        for dt in dts:
            out_shape.append(jax.ShapeDtypeStruct((t, w.shape[1]), dt))
            out_specs.append(pl.BlockSpec((tm, w.shape[1]), lambda i: (i, 0)))
    return pl.pallas_call(
        functools.partial(_inproj_kernel, plan=plan),
        grid=(t // tm,),
        in_specs=[pl.BlockSpec((tm, d), lambda i: (i, 0)),
                  pl.BlockSpec((1, d), lambda i: (0, 0))]
        + _mod_specs(kind, (0, 1), tm, tiles_per_seq, d)
        + [pl.BlockSpec(w.shape, lambda i: (0, 0)) for w in ws],
        out_specs=out_specs,
        out_shape=out_shape,
        compiler_params=_cparams(("arbitrary",)),
        name=name,
    )(x, nw, mod, mod, *ws)


ROUTE_GATE, ROUTE_IDX, ROUTE_RANK = 0, TOP_K, 2 * TOP_K
ROUTE_PAD_LOGIT = -1e30


def _route_tile(lg, cnt):
    tm = lg.shape[0]
    lane = lax.broadcasted_iota(jnp.int32, lg.shape, 1).astype(F32)
    work = lg
    vals, idxs = [], []
    for _ in range(TOP_K):
        m = jnp.max(work, axis=-1, keepdims=True)
        ix = jnp.min(jnp.where(work == m, lane, float(LANES)), axis=-1, keepdims=True)
        vals.append(m)
        idxs.append(ix)
        work = jnp.where(lane == ix, -jnp.inf, work)
    es = [jnp.exp(v - vals[0]) for v in vals]
    denom = es[0]
    for e in es[1:]:
        denom = denom + e
    onehot = jnp.zeros(lg.shape, F32)
    for ix in idxs:
        onehot = onehot + jnp.where(lane == ix, 1.0, 0.0)
    before = _tri(tm, strict=True, lower=True).astype(BF16)
    prefix = _dot(before, onehot.astype(BF16)) + cnt
    rec = jnp.zeros(lg.shape, F32)
    for j in range(TOP_K):
        rank = jnp.sum(jnp.where(lane == idxs[j], prefix, 0.0), axis=-1, keepdims=True)
        rec = jnp.where(lane == float(ROUTE_GATE + j), es[j] / denom, rec)
        rec = jnp.where(lane == float(ROUTE_IDX + j), idxs[j], rec)
        rec = jnp.where(lane == float(ROUTE_RANK + j), rank, rec)
    return rec, cnt + jnp.sum(onehot, axis=0, keepdims=True)


def _post_kernel(*refs, n_parts):
    m_refs, w_refs = refs[:n_parts], refs[n_parts:2 * n_parts]
    (x_ref, g_ref, nw_ref, sh_ref, sc_ref, rw_ref, rb_ref, cnt0_ref,
     xo_ref, h_ref, rt_ref, cnt_ref, cnt) = refs[2 * n_parts:]

    @pl.when(pl.program_id(0) == 0)
    def _():
        cnt[...] = cnt0_ref[...]

    o = _dot(m_refs[0][...], w_refs[0][...])
    for m_ref, w_ref in zip(m_refs[1:], w_refs[1:]):
        o = o + _dot(m_ref[...], w_ref[...])
    xn = x_ref[...] + g_ref[...] * o
    xo_ref[...] = xn
    h = _modulate(xn, nw_ref[...], sh_ref[...], sc_ref[...])
    h_ref[...] = h
    lg = _dot(h.astype(BF16), rw_ref[...]) + rb_ref[...]
    rec, cnt_new = _route_tile(lg, cnt[...])
    rt_ref[...] = rec
    cnt[...] = cnt_new
    cnt_ref[...] = cnt_new


def _post(parts, w_parts, x, nw2, mod, kind, rw, rb, cnt0, tiles_per_seq, name):
    t, d = x.shape
    tm = min(ROW_TILE, t)
    n_parts = len(parts)
    row = lambda n: pl.BlockSpec((tm, n), lambda i: (i, 0))
    full = lambda a: pl.BlockSpec(a.shape, lambda i: (0, 0))
    g_spec, sh_spec, sc_spec = _mod_specs(kind, (2, 3, 4), tm, tiles_per_seq, d)
    return pl.pallas_call(
        functools.partial(_post_kernel, n_parts=n_parts),
        grid=(t // tm,),
        in_specs=[row(p.shape[1]) for p in parts] + [full(w) for w in w_parts]
        + [row(d), g_spec, full(nw2), sh_spec, sc_spec, full(rw), full(rb), full(cnt0)],
        out_specs=[row(d), row(d), row(LANES), full(cnt0)],
        out_shape=[jax.ShapeDtypeStruct((t, d), F32), jax.ShapeDtypeStruct((t, d), F32),
                   jax.ShapeDtypeStruct((t, LANES), F32), jax.ShapeDtypeStruct(cnt0.shape, F32)],
        scratch_shapes=[pltpu.VMEM(cnt0.shape, F32)],
        compiler_params=_cparams(("arbitrary",)),
        name=name,
    )(*parts, *w_parts, x, mod, nw2, mod, mod, rw, rb, cnt0)


def _sb_prompt_kernel(bias_ref, q_ref, k_ref, v_ref, o_ref, acc_ref, run_ref):
    qi = pl.program_id(1)
    upper = _tri(CHUNK, strict=True, lower=True).astype(BF16)
    upper2 = jnp.concatenate([upper, upper], axis=0)
    causal = _tri(CHUNK, strict=True, lower=True)
    heads = range(SB_HEADS)
    cols = [slice(h * SB_HEAD_DIM, (h + 1) * SB_HEAD_DIM) for h in heads]

    def key_block(s0, diagonal):
        zs = [_dot_nt(q_ref[:, cols[h]], k_ref[pl.ds(s0, CHUNK), cols[h]]) for h in heads]
        ls, lks, splits = [], [], []
        for h in heads:
            z = zs[h] * (SB_HEAD_DIM ** -0.5) + bias_ref[h]
            sp = _softplus_sum(z)
            lk = -sp
            if diagonal:
                lk = jnp.where(causal, lk, 0.0)
            ls.append(z - sp)
            lks.append(lk[:, 0:1])
            splits.append(jnp.concatenate(_split_bf16(lk), axis=1))
        betweens = [_dot(splits[h], upper2) for h in heads]
        ws = []
        for h in heads:
            between = betweens[h] if diagonal else betweens[h] + run_ref[h]
            w = jnp.exp(ls[h] + between)
            if diagonal:
                w = jnp.where(causal, w, 0.0)
            ws.append(w.astype(BF16))
            run_ref[h] = between[:, 0:1] + lks[h]
        outs = [_dot(ws[h], v_ref[pl.ds(s0, CHUNK), cols[h]]) for h in heads]
        for h in heads:
            if diagonal:
                acc_ref[:, cols[h]] = outs[h]
            else:
                acc_ref[:, cols[h]] += outs[h]

    key_block(pl.multiple_of(qi * CHUNK, CHUNK), True)

    def body(t, carry):
        key_block(pl.multiple_of((qi - t) * CHUNK, CHUNK), False)
        return carry

    lax.fori_loop(1, qi + 1, body, 0)
    o_ref[...] = acc_ref[...].astype(o_ref.dtype)


def _sb_prompt(q, k, v, bias, batch, seq):
    nq = seq // CHUNK
    return pl.pallas_call(
        _sb_prompt_kernel,
        grid=(batch, nq),
        in_specs=[pl.BlockSpec(memory_space=pltpu.SMEM),
                  pl.BlockSpec((CHUNK, SB_WIDTH), lambda b, i: (b * nq + i, 0)),
                  pl.BlockSpec((seq, SB_WIDTH), lambda b, i: (b, 0)),
                  pl.BlockSpec((seq, SB_WIDTH), lambda b, i: (b, 0))],
        out_specs=pl.BlockSpec((CHUNK, SB_WIDTH), lambda b, i: (b * nq + i, 0)),
        out_shape=jax.ShapeDtypeStruct((batch * seq, SB_WIDTH), BF16),
        scratch_shapes=[pltpu.VMEM((CHUNK, SB_WIDTH), F32), pltpu.VMEM((SB_HEADS, CHUNK, 1), F32)],
        compiler_params=_cparams(("arbitrary", "arbitrary")),
        name="sb_prompt",
    )(bias, q, k, v)


def _sb_sample_kernel(pt_ref, q_ref, bias_ref, *refs, n_pages):
    k_refs, v_refs, o_ref = refs[:n_pages], refs[n_pages:2 * n_pages], refs[2 * n_pages]
    row = lax.broadcasted_iota(jnp.int32, (SB_HEADS, SB_WIDTH), 0)
    col = lax.broadcasted_iota(jnp.int32, (SB_HEADS, SB_WIDTH), 1)
    own = (col // SB_HEAD_DIM) == row
    qm = jnp.where(own, jnp.broadcast_to(q_ref[...].astype(F32), (SB_HEADS, SB_WIDTH)), 0.0).astype(BF16)
    upper = _tri(CHUNK, strict=True, lower=True).astype(BF16)
    upper2 = jnp.concatenate([upper, upper], axis=0)
    bias = bias_ref[...]
    pages = range(n_pages)
    zs = [_dot_nt(qm, k_refs[p][...]) * (SB_HEAD_DIM ** -0.5) + bias for p in pages]
    sps = [_softplus_sum(z) for z in zs]
    splits = [jnp.concatenate(_split_bf16(-sp), axis=1) for sp in sps]
    betweens = [_dot(s, upper2) for s in splits]
    run = jnp.zeros((SB_HEADS, 1), F32)
    acc = jnp.zeros((SB_HEADS, SB_WIDTH), F32)
    for p in reversed(pages):
        between = betweens[p] + run
        w = jnp.exp((zs[p] - sps[p]) + between)
        acc = acc + _dot(w.astype(BF16), v_refs[p][...])
        run = between[:, 0:1] - sps[p][:, 0:1]
    o_ref[...] = jnp.sum(jnp.where(own, acc, 0.0), axis=0, keepdims=True).astype(o_ref.dtype)


def _sb_sample(q, ck, cv, layer, page_table, bias):
    bs, n_pages = page_table.shape
    page = ck.shape[2]
    page_spec = lambda p: pl.BlockSpec((None, None, page, SB_WIDTH),
                                       lambda b, pt, p=p: (layer, pt[b * n_pages + p], 0, 0))
    grid_spec = pltpu.PrefetchScalarGridSpec(
        num_scalar_prefetch=1,
        grid=(bs,),
        in_specs=[pl.BlockSpec((None, 1, SB_WIDTH), lambda b, pt: (b, 0, 0)),
                  pl.BlockSpec((SB_HEADS, 1), lambda b, pt: (0, 0))]
        + [page_spec(p) for p in range(n_pages)] * 2,
        out_specs=pl.BlockSpec((None, 1, SB_WIDTH), lambda b, pt: (b, 0, 0)),
    )
    out = pl.pallas_call(
        functools.partial(_sb_sample_kernel, n_pages=n_pages),
        grid_spec=grid_spec,
        out_shape=jax.ShapeDtypeStruct((bs, 1, SB_WIDTH), BF16),
        compiler_params=_cparams(("arbitrary",)),
        name="sb_sample",
    )(page_table.reshape(-1), q.reshape(bs, 1, SB_WIDTH), bias.reshape(SB_HEADS, 1),
      *([ck] * n_pages), *([cv] * n_pages))
    return out.reshape(bs, SB_WIDTH)


def _log_gamma(h):
    return math.log1p(-(2.0 ** (-5.0 - h)))


def _rope_tables(pos):
    half = RET_DIM // 2
    inv = ROPE_BASE ** (-jnp.arange(half, dtype=F32) / half)
    ang = pos.astype(F32)[:, None] * inv[None, :]
    cos, sin = jnp.cos(ang), jnp.sin(ang)
    return jnp.concatenate([cos, cos], axis=-1), jnp.concatenate([-sin, sin], axis=-1)


def _rope(x, cos2, sin2):
    return x * cos2 + pltpu.roll(x, RET_DIM // 2, 1) * sin2


def _group_norm_gate(o, gn, g):
    mu = jnp.mean(o, axis=-1, keepdims=True)
    c = o - mu
    var = jnp.mean(c * c, axis=-1, keepdims=True)
    return _silu(g) * ((c * lax.rsqrt(var + EPS)) * gn)


def _ret_prompt_kernel(q_ref, k_ref, v_ref, g_ref, cos_ref, sin_ref, gn_ref, o_ref, r_ref, state):
    c = pl.program_id(1)

    @pl.when(c == 0)
    def _():
        state[...] = jnp.zeros_like(state)

    cos2, sin2 = cos_ref[...], sin_ref[...]
    li = lax.broadcasted_iota(jnp.int32, (CHUNK, CHUNK), 0)
    mi = lax.broadcasted_iota(jnp.int32, (CHUNK, CHUNK), 1)
    diff = li - mi
    idx = lax.broadcasted_iota(jnp.int32, (CHUNK, 1), 0).astype(F32)
    heads = range(RET_HEADS)
    lgs = [_log_gamma(h) for h in heads]
    cols = [slice(h * RET_DIM, (h + 1) * RET_DIM) for h in heads]
    qbs = [_rope(q_ref[:, cols[h]], cos2, sin2).astype(BF16) for h in heads]
    ks = [_rope(k_ref[:, cols[h]], cos2, sin2) * (RET_DIM ** -0.5) for h in heads]
    vs = [v_ref[:, cols[h]].astype(BF16) for h in heads]
    r_olds = [state[h] for h in heads]
    raw = [_dot_nt(qbs[h], ks[h].astype(BF16)) for h in heads]
    carried = [_dot(qbs[h], r_olds[h].astype(BF16)) for h in heads]
    grown = [_dot_tn((ks[h] * jnp.exp((CHUNK - 1.0 - idx) * lgs[h])).astype(BF16), vs[h]) for h in heads]
    scores = [raw[h] * jnp.where(diff >= 0, jnp.exp(jnp.maximum(diff, 0).astype(F32) * lgs[h]), 0.0) for h in heads]
    inner = [_dot(scores[h].astype(BF16), vs[h]) for h in heads]
    for h in heads:
        state[h] = math.exp(CHUNK * lgs[h]) * r_olds[h] + grown[h]
        from_state = carried[h] * jnp.exp((idx + 1.0) * lgs[h])
        o = _group_norm_gate(inner[h] + from_state, gn_ref[:, cols[h]], g_ref[:, cols[h]])
        o_ref[:, cols[h]] = o.astype(o_ref.dtype)

    @pl.when(c == pl.num_programs(1) - 1)
    def _():
        r_ref[...] = state[...]


def _ret_prompt(ret, gn_w, batch, seq):
    nc = seq // CHUNK
    cos2, sin2 = _rope_tables(jnp.arange(seq, dtype=jnp.int32))
    col = lambda j: pl.BlockSpec((CHUNK, RET_WIDTH), lambda b, c, j=j: (b * nc + c, j))
    tab = pl.BlockSpec((CHUNK, RET_DIM), lambda b, c: (c, 0))
    return pl.pallas_call(
        _ret_prompt_kernel,
        grid=(batch, nc),
        in_specs=[col(0), col(1), col(2), col(3), tab, tab, pl.BlockSpec((1, RET_WIDTH), lambda b, c: (0, 0))],
        out_specs=[pl.BlockSpec((CHUNK, RET_WIDTH), lambda b, c: (b * nc + c, 0)),
                   pl.BlockSpec((None, RET_HEADS, RET_DIM, RET_DIM), lambda b, c: (b, 0, 0, 0))],
        out_shape=[jax.ShapeDtypeStruct((batch * seq, RET_WIDTH), BF16),
                   jax.ShapeDtypeStruct((batch, RET_HEADS, RET_DIM, RET_DIM), F32)],
        scratch_shapes=[pltpu.VMEM((RET_HEADS, RET_DIM, RET_DIM), F32)],
        compiler_params=_cparams(("arbitrary", "arbitrary")),
        name="ret_prompt",
    )(ret, ret, ret, ret, cos2, sin2, gn_w)


RET_SEQS = 8


def _ret_sample_kernel(q_ref, k_ref, v_ref, g_ref, cos_ref, sin_ref, gn_ref, r0_ref, o_ref, r_ref):
    cos2, sin2 = cos_ref[...], sin_ref[...]
    for h in range(RET_HEADS):
        gamma = math.exp(_log_gamma(h))
        cols = slice(h * RET_DIM, (h + 1) * RET_DIM)
        q = _rope(q_ref[:, cols], cos2, sin2)
        k = _rope(k_ref[:, cols], cos2, sin2) * (RET_DIM ** -0.5)
        v = v_ref[:, cols]
        qb, kb, vb = q.astype(BF16), k.astype(BF16), v.astype(BF16)
        qk = jnp.sum(qb.astype(F32) * kb.astype(F32), axis=-1, keepdims=True)
        inner = qk * vb.astype(F32)
        rows = []
        for s in range(RET_SEQS):
            r_old = r0_ref[s, h]
            rows.append(_dot(qb[s:s + 1, :], r_old.astype(BF16)) * gamma)
            k_col = jnp.transpose(kb[s:s + 1, :].astype(F32))
            r_ref[s, h] = gamma * r_old + k_col * vb[s:s + 1, :].astype(F32)
        o = inner + jnp.concatenate(rows, axis=0)
        o_ref[:, cols] = _group_norm_gate(o, gn_ref[:, cols], g_ref[:, cols]).astype(o_ref.dtype)


def _ret_sample(ret, gn_w, r0, layer, past_len):
    bs = ret.shape[0]
    cos2, sin2 = _rope_tables(jnp.full((1,), past_len, jnp.int32))
    col = lambda j: pl.BlockSpec((RET_SEQS, RET_WIDTH), lambda i, j=j: (i, j))
    one = lambda n: pl.BlockSpec((1, n), lambda i: (0, 0))
    st_in = pl.BlockSpec((None, RET_SEQS, RET_HEADS, RET_DIM, RET_DIM), lambda i: (layer, i, 0, 0, 0))
    st = pl.BlockSpec((RET_SEQS, RET_HEADS, RET_DIM, RET_DIM), lambda i: (i, 0, 0, 0))
    return pl.pallas_call(
        _ret_sample_kernel,
        grid=(bs // RET_SEQS,),
        in_specs=[col(0), col(1), col(2), col(3), one(RET_DIM), one(RET_DIM), one(RET_WIDTH), st_in],
        out_specs=[pl.BlockSpec((RET_SEQS, RET_WIDTH), lambda i: (i, 0)), st],
        out_shape=[jax.ShapeDtypeStruct((bs, RET_WIDTH), BF16),
                   jax.ShapeDtypeStruct(r0.shape[1:], F32)],
        compiler_params=_cparams(("arbitrary",)),
        name="ret_sample",
    )(ret, ret, ret, ret, cos2, sin2, gn_w, r0)


def _gated_group_rms(y, z, nw, groups):
    yg = y * _silu(z)
    gw = y.shape[-1] // groups
    outs = []
    for g in range(groups):
        a = yg[:, g * gw:(g + 1) * gw]
        outs.append(a * lax.rsqrt(jnp.mean(a * a, axis=-1, keepdims=True) + EPS))
    return jnp.concatenate(outs, axis=-1) * nw


def _ssd_prompt_kernel(z_ref, xbc_ref, dt_ref, cw_ref, cb_ref, dtb_ref, alog_ref, dsk_ref, nw_ref, sel_ref,
                       y_ref, s_ref, cv_ref, state, win, ybuf, *, heads, width):
    c = pl.program_id(1)
    hpg = heads // SSD_GROUPS
    gn = SSD_GROUPS * SSD_STATE

    @pl.when(c == 0)
    def _():
        state[...] = jnp.zeros_like(state)
        win[0:8, :] = jnp.zeros((8, win.shape[1]), F32)

    win[8:8 + CHUNK, :] = xbc_ref[...]
    conv = cb_ref[...]
    for j in range(SSD_CONV):
        conv = conv + win[5 + j:5 + j + CHUNK, :] * cw_ref[j:j + 1, :]
    act = _silu(conv)

    @pl.when(c == pl.num_programs(1) - 1)
    def _():
        cv_ref[...] = win[CHUNK + 5:CHUNK + 8, :]

    win[0:8, :] = win[CHUNK:CHUNK + 8, :]

    dt = _softplus(dt_ref[...] + dtb_ref[...])
    a = -jnp.exp(alog_ref[...])
    da_hi, da_lo = _split_bf16(dt * a)
    lower = _tri(CHUNK, strict=False, lower=True).astype(BF16)
    acs = _dot(lower, da_hi) + _dot(lower, da_lo)
    acs_t = jnp.transpose(acs)
    dt_t = jnp.transpose(dt)
    both = jnp.concatenate([acs, dt], axis=0)
    p0 = both.astype(BF16)
    r0 = both - p0.astype(F32)
    p1 = r0.astype(BF16)
    p2 = (r0 - p1.astype(F32)).astype(BF16)
    wide = _dot(jnp.concatenate([p0, p1, p2], axis=1), sel_ref[...])
    causal = _tri(CHUNK, strict=False, lower=True)
    p = SSD_HEAD_DIM
    for g in range(SSD_GROUPS):
        bg = act[:, width + g * SSD_STATE:width + (g + 1) * SSD_STATE].astype(BF16)
        cg = act[:, width + gn + g * SSD_STATE:width + gn + (g + 1) * SSD_STATE].astype(BF16)
        cbm = _dot_nt(cg, bg)
        for r in range(hpg):
            h = g * hpg + r
            cols = slice(h * p, (h + 1) * p)
            xh = act[:, cols]
            acs_b = wide[:CHUNK, h * LANES:(h + 1) * LANES]
            dt_b = wide[CHUNK:, h * LANES:h * LANES + p]
            seg = jnp.where(causal, acs_b - acs_t[h:h + 1, :], -jnp.inf)
            w = cbm * jnp.exp(seg) * dt_t[h:h + 1, :]
            s_old = state[h]
            y = _dot(w.astype(BF16), xh.astype(BF16))
            y = y + _dot_nt(cg, s_old.astype(BF16)) * jnp.exp(acs_b[:, :p])
            last = acs_b[CHUNK - 1:CHUNK, :]
            to_end = jnp.exp(last[:, :p] - acs_b[:, :p]) * dt_b
            state[h] = jnp.exp(last) * s_old + _dot_tn((xh * to_end).astype(BF16), bg)
            ybuf[:, cols] = y + dsk_ref[:, cols] * xh

    y_ref[...] = _gated_group_rms(ybuf[...], z_ref[...], nw_ref[...], SSD_GROUPS).astype(y_ref.dtype)

    @pl.when(c == pl.num_programs(1) - 1)
    def _():
        s_ref[...] = state[...]


def _ssd_prompt(z, xbc, dt, cw, cb, dtb, alog, dsk, nw, batch, seq, heads):
    nc = seq // CHUNK
    width = z.shape[1]
    cdim = xbc.shape[1]
    row = lambda n: pl.BlockSpec((CHUNK, n), lambda b, c: (b * nc + c, 0))
    full = lambda a: pl.BlockSpec(a.shape, lambda b, c: (0, 0))
    lane = jnp.arange(LANES, dtype=jnp.int32)[:, None]
    tile = jnp.arange(heads * LANES, dtype=jnp.int32)[None, :] // LANES
    sel = jnp.tile((lane == tile).astype(BF16), (3, 1))
    return pl.pallas_call(
        functools.partial(_ssd_prompt_kernel, heads=heads, width=width),
        grid=(batch, nc),
        in_specs=[row(width), row(cdim), row(LANES), full(cw), full(cb), full(dtb), full(alog), full(dsk), full(nw),
                  full(sel)],
        out_specs=[row(width),
                   pl.BlockSpec((None, heads, SSD_HEAD_DIM, SSD_STATE), lambda b, c: (b, 0, 0, 0)),
                   pl.BlockSpec((None, SSD_CONV - 1, cdim), lambda b, c: (b, 0, 0))],
        out_shape=[jax.ShapeDtypeStruct((batch * seq, width), BF16),
                   jax.ShapeDtypeStruct((batch, heads, SSD_HEAD_DIM, SSD_STATE), F32),
                   jax.ShapeDtypeStruct((batch, SSD_CONV - 1, cdim), F32)],
        scratch_shapes=[pltpu.VMEM((heads, SSD_HEAD_DIM, SSD_STATE), F32),
                        pltpu.VMEM((CHUNK + 8, cdim), F32),
                        pltpu.VMEM((CHUNK, width), F32)],
        compiler_params=_cparams(("arbitrary", "arbitrary")),
        name="ssd_prompt",
    )(z, xbc, dt, cw, cb, dtb, alog, dsk, nw, sel)


SSD_SEQS = 4


def _ssd_sample_kernel(z_ref, xbc_ref, dt_ref, cprev_ref, cw_ref, cb_ref, dtb_ref, alog_ref, dsk_ref, nw_ref,
                       s0_ref, y_ref, s_ref, cv_ref, *, heads, width):
    hpg = heads // SSD_GROUPS
    gn = SSD_GROUPS * SSD_STATE
    for s in range(SSD_SEQS):
        dt = _softplus(dt_ref[s] + dtb_ref[...])
        decay = jnp.exp(dt * (-jnp.exp(alog_ref[...])))
        xrow = xbc_ref[s]
        prev = cprev_ref[s]
        conv = cb_ref[...] + xrow * cw_ref[SSD_CONV - 1:SSD_CONV, :]
        for j in range(SSD_CONV - 1):
            conv = conv + prev[j:j + 1, :] * cw_ref[j:j + 1, :]
        cv_ref[s, 0:SSD_CONV - 2, :] = prev[1:SSD_CONV - 1, :]
        cv_ref[s, SSD_CONV - 2:SSD_CONV - 1, :] = xrow
        act = _silu(conv)
        yrow = []
        for g in range(SSD_GROUPS):
            bg = act[:, width + g * SSD_STATE:width + (g + 1) * SSD_STATE].astype(BF16).astype(F32)
            cg = act[:, width + gn + g * SSD_STATE:width + gn + (g + 1) * SSD_STATE].astype(BF16)
            for r in range(hpg):
                h = g * hpg + r
                cols = slice(h * SSD_HEAD_DIM, (h + 1) * SSD_HEAD_DIM)
                xh = act[:, cols]
                dtx = (xh * dt[:, h:h + 1]).astype(BF16).astype(F32)
                s_new = decay[:, h:h + 1] * s0_ref[s, h] + jnp.transpose(dtx) * bg
                s_ref[s, h] = s_new
                yrow.append(_dot_nt(cg, s_new.astype(BF16)) + dsk_ref[:, cols] * xh)
        y = jnp.concatenate(yrow, axis=-1)
        y_ref[s] = _gated_group_rms(y, z_ref[s], nw_ref[...], SSD_GROUPS).astype(y_ref.dtype)


def _ssd_sample(z, xbc, dt, conv_prev, s0, layer, cw, cb, dtb, alog, dsk, nw, heads):
    bs, width = z.shape
    cdim = xbc.shape[1]
    row = lambda n: pl.BlockSpec((SSD_SEQS, 1, n), lambda i: (i, 0, 0))
    full = lambda a: pl.BlockSpec(a.shape, lambda i: (0, 0))
    st_in = pl.BlockSpec((None, SSD_SEQS, heads, SSD_HEAD_DIM, SSD_STATE), lambda i: (layer, i, 0, 0, 0))
    cv_in = pl.BlockSpec((None, SSD_SEQS, SSD_CONV - 1, cdim), lambda i: (layer, i, 0, 0))
    st = pl.BlockSpec((SSD_SEQS, heads, SSD_HEAD_DIM, SSD_STATE), lambda i: (i, 0, 0, 0))
    cv = pl.BlockSpec((SSD_SEQS, SSD_CONV - 1, cdim), lambda i: (i, 0, 0))
    y, s_new, cv_new = pl.pallas_call(
        functools.partial(_ssd_sample_kernel, heads=heads, width=width),
        grid=(bs // SSD_SEQS,),
        in_specs=[row(width), row(cdim), row(LANES), cv_in, full(cw), full(cb), full(dtb), full(alog), full(dsk),
                  full(nw), st_in],
        out_specs=[row(width), st, cv],
        out_shape=[jax.ShapeDtypeStruct((bs, 1, width), BF16),
                   jax.ShapeDtypeStruct(s0.shape[1:], F32),
                   jax.ShapeDtypeStruct(conv_prev.shape[1:], F32)],
        compiler_params=_cparams(("arbitrary",)),
        name="ssd_sample",
    )(z.reshape(bs, 1, width), xbc.reshape(bs, 1, cdim), dt.reshape(bs, 1, LANES), conv_prev, cw, cb, dtb, alog,
      dsk, nw, s0)
    return y.reshape(bs, width), s_new, cv_new


def _expert_kernel(be_ref, nu_ref, x_ref, w1_ref, b1_ref, w2_ref, b2_ref, o_ref, w1s, w2s, *, f):
    i = pl.program_id(0)
    prev = be_ref[jnp.maximum(i - 1, 0)]

    @pl.when((i == 0) | (be_ref[i] != prev))
    def _():
        w1s[...] = w1_ref[...].astype(BF16)
        w2s[...] = w2_ref[...].astype(BF16)

    @pl.when(i < nu_ref[0])
    def _():
        x = x_ref[...].astype(BF16)
        acc = jnp.zeros(o_ref.shape, F32) + b2_ref[...]
        half = f // 2
        for c in range(2):
            glu = _dot(x, w1s[:, c * half:(c + 1) * half]) + b1_ref[:, c * half:(c + 1) * half]
            lin = _dot(x, w1s[:, f + c * half:f + (c + 1) * half]) + b1_ref[:, f + c * half:f + (c + 1) * half]
            glu = jnp.minimum(glu, SWIGLU_LIMIT)
            lin = jnp.clip(lin, -SWIGLU_LIMIT, SWIGLU_LIMIT)
            a = glu * jax.nn.sigmoid(SWIGLU_ALPHA * glu) * (lin + 1.0)
            acc = acc + _dot(a.astype(BF16), w2s[c * half:(c + 1) * half, :])
        o_ref[...] = acc

    @pl.when(i >= nu_ref[0])
    def _():
        o_ref[...] = jnp.zeros_like(o_ref)


def _experts(block_expert, n_used, xb, layer, w1, b1, w2, b2):
    n_rows = xb.shape[0]
    _, n_exp, d, f2 = w1.shape
    f = f2 // 2
    n_blocks = n_rows // MOE_ROWS
    rows = pl.BlockSpec((MOE_ROWS, d), lambda i, be, nu: (i, 0))
    grid_spec = pltpu.PrefetchScalarGridSpec(
        num_scalar_prefetch=2,
        grid=(n_blocks,),
        in_specs=[rows,
                  pl.BlockSpec((None, None, d, f2), lambda i, be, nu: (layer, be[i], 0, 0)),
                  pl.BlockSpec((None, None, 1, f2), lambda i, be, nu: (layer, be[i], 0, 0)),
                  pl.BlockSpec((None, None, f, d), lambda i, be, nu: (layer, be[i], 0, 0)),
                  pl.BlockSpec((None, None, 1, d), lambda i, be, nu: (layer, be[i], 0, 0))],
        out_specs=rows,
        scratch_shapes=[pltpu.VMEM((d, f2), BF16), pltpu.VMEM((f, d), BF16)],
    )
    depth = w1.shape[0]
    return pl.pallas_call(
        functools.partial(_expert_kernel, f=f),
        grid_spec=grid_spec,
        out_shape=jax.ShapeDtypeStruct(xb.shape, F32),
        compiler_params=_cparams(("arbitrary",)),
        name="moe_experts",
    )(block_expert, n_used, xb, w1, b1.reshape(depth, n_exp, 1, f2), w2, b2.reshape(depth, n_exp, 1, d))


def _expert_layout(counts, n_blocks):
    n_exp = counts.shape[0]
    padded = (counts + MOE_ROWS - 1) // MOE_ROWS * MOE_ROWS
    pad_end = jnp.cumsum(padded)
    starts = jnp.arange(n_blocks, dtype=jnp.int32) * MOE_ROWS
    block_expert = jnp.minimum(jnp.sum(starts[:, None] >= pad_end[None, :], axis=1), n_exp - 1).astype(jnp.int32)
    return pad_end - padded, block_expert, (pad_end[-1] // MOE_ROWS).astype(jnp.int32).reshape(1)


def _row_positions(rec, pad_start):
    n_exp = pad_start.shape[0]
    idx = rec[:, ROUTE_IDX:ROUTE_IDX + TOP_K].astype(jnp.int32)
    rank = rec[:, ROUTE_RANK:ROUTE_RANK + TOP_K].astype(jnp.int32)
    start = jnp.sum(jnp.where(idx[..., None] == jnp.arange(n_exp, dtype=jnp.int32), pad_start, 0), axis=-1)
    return (start + rank).reshape(-1).astype(jnp.int32)


SUBLANES = 8
COMBINE_ROWS = 32


def _dispatch_kernel(pos_ref, h_ref, xb_in, xb_ref, sem):
    del xb_in
    tm = h_ref.shape[0]
    base = pl.program_id(0) * (tm * TOP_K)

    def row_copy(g, k, j):
        t0 = pl.multiple_of(g * SUBLANES, SUBLANES)
        dst = pos_ref[base + g * (SUBLANES * TOP_K) + (k * TOP_K + j)]
        return pltpu.make_async_copy(h_ref.at[pl.ds(t0 + k, 1), :], xb_ref.at[pl.ds(dst, 1), :], sem)

    def issue(g, carry):
        for k in range(SUBLANES):
            for j in range(TOP_K):
                row_copy(g, k, j).start()
        return carry

    def drain(g, carry):
        for k in range(SUBLANES):
            for j in range(TOP_K):
                row_copy(g, k, j).wait()
        return carry

    lax.fori_loop(0, tm // SUBLANES, issue, 0)
    lax.fori_loop(0, tm // SUBLANES, drain, 0)


def _dispatch(pos, h, xb, name):
    t, d = h.shape
    tm = min(ROW_TILE, t)
    grid_spec = pltpu.PrefetchScalarGridSpec(
        num_scalar_prefetch=1,
        grid=(t // tm,),
        in_specs=[pl.BlockSpec((tm, d), lambda i, p: (i, 0)),
                  pl.BlockSpec(memory_space=pl.ANY)],
        out_specs=pl.BlockSpec(memory_space=pl.ANY),
        scratch_shapes=[pltpu.SemaphoreType.DMA],
    )
    return pl.pallas_call(
        _dispatch_kernel,
        grid_spec=grid_spec,
        out_shape=jax.ShapeDtypeStruct(xb.shape, xb.dtype),
        input_output_aliases={2: 0},
        compiler_params=_cparams(("arbitrary",)),
        name=name,
    )(pos, h, xb)


def _combine_kernel(pos_ref, x_ref, g_ref, rec_ref, yb_ref, *rest, final):
    if final:
        fw_ref, o_ref, ybuf, sem = rest
    else:
        o_ref, ybuf, sem = rest
    tm = x_ref.shape[0]
    base = pl.program_id(0) * (tm * TOP_K)

    def row_copy(g, k, j):
        t0 = pl.multiple_of(g * SUBLANES, SUBLANES)
        src = pos_ref[base + g * (SUBLANES * TOP_K) + (k * TOP_K + j)]
        return pltpu.make_async_copy(yb_ref.at[pl.ds(src, 1), :], ybuf.at[j, pl.ds(t0 + k, 1), :], sem)

    def issue(g, carry):
        for k in range(SUBLANES):
            for j in range(TOP_K):
                row_copy(g, k, j).start()
        return carry

    def drain(g, carry):
        for k in range(SUBLANES):
            for j in range(TOP_K):
                row_copy(g, k, j).wait()
        return carry

    lax.fori_loop(0, tm // SUBLANES, issue, 0)
    lax.fori_loop(0, tm // SUBLANES, drain, 0)

    def rows(c, carry):
        r = pl.ds(pl.multiple_of(c * COMBINE_ROWS, COMBINE_ROWS), COMBINE_ROWS)
        rec = rec_ref[r, :]
        y = rec[:, ROUTE_GATE:ROUTE_GATE + 1] * ybuf[0, r, :]
        for j in range(1, TOP_K):
            y = y + rec[:, ROUTE_GATE + j:ROUTE_GATE + j + 1] * ybuf[j, r, :]
        g = g_ref[...] if g_ref.shape[0] == 1 else g_ref[r, :]
        xn = x_ref[r, :] + g * y
        if final:
            xn = (xn * lax.rsqrt(jnp.mean(xn * xn, axis=-1, keepdims=True) + EPS)) * fw_ref[...]
        o_ref[r, :] = xn
        return carry

    lax.fori_loop(0, tm // COMBINE_ROWS, rows, 0)


def _combine(pos, x, mod, kind, rec, yb, tiles_per_seq, final_w, name):
    t, d = x.shape
    tm = min(ROW_TILE, t)
    (g_spec,) = _mod_specs(kind, (5,), tm, tiles_per_seq, d)
    with_p = lambda spec: pl.BlockSpec(spec.block_shape, lambda i, p, f=spec.index_map: f(i))
    in_specs = [pl.BlockSpec((tm, d), lambda i, p: (i, 0)), with_p(g_spec),
                pl.BlockSpec((tm, LANES), lambda i, p: (i, 0)),
                pl.BlockSpec(memory_space=pl.ANY)]
    args = [pos, x, mod, rec, yb]
    if final_w is not None:
        in_specs.append(pl.BlockSpec((1, d), lambda i, p: (0, 0)))
        args.append(final_w)
    grid_spec = pltpu.PrefetchScalarGridSpec(
        num_scalar_prefetch=1,
        grid=(t // tm,),
        in_specs=in_specs,
        out_specs=pl.BlockSpec((tm, d), lambda i, p: (i, 0)),
        scratch_shapes=[pltpu.VMEM((TOP_K, tm, d), F32), pltpu.SemaphoreType.DMA],
    )
    return pl.pallas_call(
        functools.partial(_combine_kernel, final=final_w is not None),
        grid_spec=grid_spec,
        out_shape=jax.ShapeDtypeStruct((t, d), F32),
        compiler_params=_cparams(("arbitrary",)),
        name=name,
    )(*args)


def kernel(x_prompt, x_sample, cache_sb_k, cache_sb_v, page_table, state_ret, state_ssd, state_conv,
           c_prompt, c_sample, norm1_w, norm2_w, w_ada, b_ada, ab_w_in, ab_w_out, ret_gn_w, sb_bias,
           ssd_w_in, ssd_conv_w, ssd_conv_b, ssd_dt_bias, ssd_a_log, ssd_d, ssd_norm_w, ssd_w_out,
           router_w, router_b, moe_w1, moe_b1, moe_w2, moe_b2, final_norm_w):
    bp, sp, d = x_prompt.shape
    bs, ls, _ = x_sample.shape
    assert ls == 1 and sp % CHUNK == 0 and cache_sb_k.shape[2] == CHUNK
    depth = w_ada.shape[0]
    n_exp = router_w.shape[-1]
    tp = bp * sp
    past_len = page_table.shape[1] * cache_sb_k.shape[2]
    ssd_width = ssd_norm_w.shape[-1]
    ssd_heads = ssd_a_log.shape[-1]
    cdim = ssd_conv_w.shape[-1]
    tps = sp // min(ROW_TILE, tp)
    assert d % LANES == 0 and n_exp <= LANES
    n_blocks = -(-((tp + bs) * TOP_K) // MOE_ROWS) + n_exp

    mod = _ada_mod(jnp.concatenate([c_prompt, c_sample], axis=0), w_ada, b_ada)
    xp = x_prompt.reshape(tp, d)
    xs = x_sample.reshape(bs, d)
    row1 = lambda a: a.reshape(1, -1)
    pad_l = lambda a: jnp.pad(a, ((0, 0), (0, LANES - a.shape[-1])))

    sbk_p, sbv_p, sbk_s, sbv_s, ret_p, ret_s = [], [], [], [], [], []
    ssd_p, ssd_s, conv_p, conv_s = [], [], [], []
    y_prompt = y_sample = None
    xb = jnp.zeros((n_blocks * MOE_ROWS, d), F32)
    pool = cache_sb_k.shape[:3] + (SB_WIDTH,)
    ck = cache_sb_k.astype(BF16).reshape(pool)
    cv = cache_sb_v.astype(BF16).reshape(pool)
    for layer in range(depth):
        i = layer // 2
        modp = mod[layer, :bp].reshape(bp * 6, 1, d)
        mods = mod[layer, bp:]
        nw1 = row1(norm1_w[layer])
        if layer % 2 == 0:
            w_in = ab_w_in[i].astype(BF16)
            ws = [w_in[:, :SB_WIDTH], w_in[:, SB_WIDTH:2 * SB_WIDTH], w_in[:, 2 * SB_WIDTH:3 * SB_WIDTH],
                  w_in[:, 3 * SB_WIDTH:]]
            plan = ((BF16,), (F32, BF16), (F32, BF16), (F32,))
            qa_p, ka_p, kab_p, va_p, vab_p, rt_p = _inproj(xp, nw1, modp, "p", ws, plan, tps, "ab_in_prompt")
            qa_s, ka_s, _, va_s, _, rt_s = _inproj(xs, nw1, mods, "s", ws, plan, 1, "ab_in_sample")
            gn = row1(ret_gn_w[i])
            o_sb_p = _sb_prompt(qa_p, kab_p, vab_p, sb_bias[i], bp, sp)
            o_rt_p, r_p = _ret_prompt(rt_p, gn, bp, sp)
            o_sb_s = _sb_sample(qa_s, ck, cv, i, page_table, sb_bias[i])
            o_rt_s, r_s = _ret_sample(rt_s, gn, state_ret, i, past_len)
            parts_p, parts_s = [o_sb_p, o_rt_p], [o_sb_s, o_rt_s]
            w_out = ab_w_out[i].astype(BF16)
            w_parts = [w_out[:SB_WIDTH], w_out[SB_WIDTH:]]
            sbk_p.append(ka_p.reshape(bp, sp, SB_HEADS, SB_HEAD_DIM))
            sbv_p.append(va_p.reshape(bp, sp, SB_HEADS, SB_HEAD_DIM))
            sbk_s.append(ka_s.reshape(bs, 1, SB_HEADS, SB_HEAD_DIM))
            sbv_s.append(va_s.reshape(bs, 1, SB_HEADS, SB_HEAD_DIM))
            ret_p.append(r_p)
            ret_s.append(r_s)
        else:
            w_in = ssd_w_in[i]
            ws = [w_in[:, :ssd_width].astype(BF16), w_in[:, ssd_width:ssd_width + cdim].astype(BF16),
                  pad_l(w_in[:, ssd_width + cdim:]).astype(BF16)]
            plan = ((F32,), (F32,), (F32,))
            z_p, xbc_p, dt_p = _inproj(xp, nw1, modp, "p", ws, plan, tps, "ssd_in_prompt")
            z_s, xbc_s, dt_s = _inproj(xs, nw1, mods, "s", ws, plan, 1, "ssd_in_sample")
            consts = (ssd_conv_w[i], row1(ssd_conv_b[i]), pad_l(row1(ssd_dt_bias[i])), pad_l(row1(ssd_a_log[i])),
                      row1(jnp.repeat(ssd_d[i], SSD_HEAD_DIM)), row1(ssd_norm_w[i]))
            mixed_p, s_p, cv_p = _ssd_prompt(z_p, xbc_p, dt_p, *consts, bp, sp, ssd_heads)
            mixed_s, s_s, cv_s = _ssd_sample(z_s, xbc_s, dt_s, state_conv, state_ssd, i, *consts, ssd_heads)
            parts_p, parts_s = [mixed_p], [mixed_s]
            w_parts = [ssd_w_out[i].astype(BF16)]
            ssd_p.append(s_p)
            ssd_s.append(s_s)
            conv_p.append(cv_p)
            conv_s.append(cv_s)

        rw = pad_l(router_w[layer]).astype(BF16)
        rb = jnp.pad(row1(router_b[layer]), ((0, 0), (0, LANES - n_exp)), constant_values=ROUTE_PAD_LOGIT)
        nw2 = row1(norm2_w[layer])
        cnt0 = jnp.zeros((1, LANES), F32)
        xp, h2p, rec_p, cnt_p = _post(parts_p, w_parts, xp, nw2, modp, "p", rw, rb, cnt0, tps, "post_prompt")
        xs, h2s, rec_s, cnt = _post(parts_s, w_parts, xs, nw2, mods, "s", rw, rb, cnt_p, 1, "post_sample")

        pad_start, block_expert, n_used = _expert_layout(cnt[0, :n_exp].astype(jnp.int32), n_blocks)
        pos_p = _row_positions(rec_p, pad_start)
        pos_s = _row_positions(rec_s, pad_start)
        xb = _dispatch(pos_p, h2p, xb, "dispatch_prompt")
        xb = _dispatch(pos_s, h2s, xb, "dispatch_sample")
        yb = _experts(block_expert, n_used, xb, layer, moe_w1, moe_b1, moe_w2, moe_b2)
        last = layer == depth - 1
        fw = row1(final_norm_w) if last else None
        xp = _combine(pos_p, xp, modp, "p", rec_p, yb, tps, fw, "combine_prompt")
        xs = _combine(pos_s, xs, mods, "s", rec_s, yb, 1, fw, "combine_sample")

    st = lambda lst: jnp.stack(lst, axis=0)
    return (xp.reshape(bp, sp, d), xs.reshape(bs, ls, d), st(sbk_p), st(sbv_p), st(sbk_s), st(sbv_s),
            st(ret_p), st(ret_s), st(ssd_p), st(ssd_s), st(conv_p), st(conv_s))
```

```python
import functools
import math

import jax
import jax.numpy as jnp
from jax import lax
from jax.experimental import pallas as pl
from jax.experimental.pallas import tpu as pltpu

F32 = jnp.float32
BF16 = jnp.bfloat16

EPS = 1e-6
ROPE_BASE = 10000.0
SB_HEADS = 8
SB_HEAD_DIM = 64
SB_WIDTH = SB_HEADS * SB_HEAD_DIM
RET_HEADS = 4
RET_DIM = 128
RET_WIDTH = RET_HEADS * RET_DIM
SSD_HEAD_DIM = 64
SSD_GROUPS = 4
SSD_STATE = 128
SSD_CONV = 4
TOP_K = 4
SWIGLU_LIMIT = 7.0
SWIGLU_ALPHA = 1.702
CHUNK = 128
LANES = 128
ROW_TILE = 256
MOE_ROWS = 256
VMEM_LIMIT = 56 * 1024 * 1024


def _cparams(sem):
    return pltpu.CompilerParams(dimension_semantics=sem, vmem_limit_bytes=VMEM_LIMIT)


def _silu(x):
    return x * jax.nn.sigmoid(x)


def _softplus(x):
    return jnp.maximum(x, 0.0) + jnp.log1p(jnp.exp(-jnp.abs(x)))


def _softplus_sum(x):
    return jnp.maximum(x, 0.0) + jnp.log(1.0 + jnp.exp(-jnp.abs(x)))


def _modulate(x, nw, shift, scale):
    y = x * lax.rsqrt(jnp.mean(x * x, axis=-1, keepdims=True) + EPS)
    return (y * nw) * (1.0 + scale) + shift


def _split_bf16(x):
    hi = x.astype(BF16)
    lo = (x - hi.astype(F32)).astype(BF16)
    return hi, lo


def _dot(a, b):
    return jnp.dot(a, b, preferred_element_type=F32)


def _dot_nt(a, b):
    return lax.dot_general(a, b, (((1,), (1,)), ((), ())), preferred_element_type=F32)


def _dot_tn(a, b):
    return lax.dot_general(a, b, (((0,), (0,)), ((), ())), preferred_element_type=F32)


def _tri(n, strict, lower):
    r = lax.broadcasted_iota(jnp.int32, (n, n), 0)
    c = lax.broadcasted_iota(jnp.int32, (n, n), 1)
    if lower:
        m = (c < r) if strict else (c <= r)
    else:
        m = (c > r) if strict else (c >= r)
    return m


def _mod_specs(kind, ks, tm, tiles_per_seq, d):
    if kind == "p":
        return [pl.BlockSpec((None, 1, d), lambda i, k=k: ((i // tiles_per_seq) * 6 + k, 0, 0)) for k in ks]
    return [pl.BlockSpec((tm, d), lambda i, k=k: (i, k)) for k in ks]


def _ada_kernel(c_ref, w_ref, b_ref, o_ref):
    c = _silu(c_ref[...]).astype(BF16)
    o_ref[...] = _dot(c, w_ref[...].astype(BF16)) + b_ref[...]


def _ada_mod(c, w_ada, b_ada):
    depth, d, n = w_ada.shape
    rows = c.shape[0]
    tn = 1024
    return pl.pallas_call(
        _ada_kernel,
        grid=(depth, n // tn),
        in_specs=[pl.BlockSpec((rows, d), lambda l, j: (0, 0)),
                  pl.BlockSpec((None, d, tn), lambda l, j: (l, 0, j)),
                  pl.BlockSpec((None, 1, tn), lambda l, j: (l, 0, j))],
        out_specs=pl.BlockSpec((None, rows, tn), lambda l, j: (l, 0, j)),
        out_shape=jax.ShapeDtypeStruct((depth, rows, n), F32),
        compiler_params=_cparams(("arbitrary", "arbitrary")),
        name="ada_mod",
    )(c, w_ada, b_ada.reshape(depth, 1, n))


def _inproj_kernel(x_ref, nw_ref, sh_ref, sc_ref, *refs, plan):
    n_w = len(plan)
    w_refs, o_refs = refs[:n_w], refs[n_w:]
    h = _modulate(x_ref[...], nw_ref[...], sh_ref[...], sc_ref[...]).astype(BF16)
    oi = 0
    for w_ref, dts in zip(w_refs, plan):
        y = _dot(h, w_ref[...])
        for dt in dts:
            o_refs[oi][...] = y.astype(dt)
            oi += 1


def _inproj(x, nw, mod, kind, ws, plan, tiles_per_seq, name):
    t, d = x.shape
    tm = min(ROW_TILE, t)
    out_shape, out_specs = [], []
    for w, dts in zip(ws, plan):
        for dt in dts:
            out_shape.append(jax.ShapeDtypeStruct((t, w.shape[1]), dt))
            out_specs.append(pl.BlockSpec((tm, w.shape[1]), lambda i: (i, 0)))
    return pl.pallas_call(
        functools.partial(_inproj_kernel, plan=plan),
        grid=(t // tm,),
        in_specs=[pl.BlockSpec((tm, d), lambda i: (i, 0)),
                  pl.BlockSpec((1, d), lambda i: (0, 0))]
        + _mod_specs(kind, (0, 1), tm, tiles_per_seq, d)
        + [pl.BlockSpec(w.shape, lambda i: (0, 0)) for w in ws],
        out_specs=out_specs,
        out_shape=out_shape,
        compiler_params=_cparams(("arbitrary",)),
        name=name,
    )(x, nw, mod, mod, *ws)


ROUTE_GATE, ROUTE_IDX, ROUTE_RANK = 0, TOP_K, 2 * TOP_K
ROUTE_PAD_LOGIT = -1e30


def _route_tile(lg, cnt):
    tm = lg.shape[0]
    lane = lax.broadcasted_iota(jnp.int32, lg.shape, 1).astype(F32)
    work = lg
    vals, idxs = [], []
    for _ in range(TOP_K):
        m = jnp.max(work, axis=-1, keepdims=True)
        ix = jnp.min(jnp.where(work == m, lane, float(LANES)), axis=-1, keepdims=True)
        vals.append(m)
        idxs.append(ix)
        work = jnp.where(lane == ix, -jnp.inf, work)
    es = [jnp.exp(v - vals[0]) for v in vals]
    denom = es[0]
    for e in es[1:]:
        denom = denom + e
    onehot = jnp.zeros(lg.shape, F32)
    for ix in idxs:
        onehot = onehot + jnp.where(lane == ix, 1.0, 0.0)
    before = _tri(tm, strict=True, lower=True).astype(BF16)
    prefix = _dot(before, onehot.astype(BF16)) + cnt
    rec = jnp.zeros(lg.shape, F32)
    for j in range(TOP_K):
        rank = jnp.sum(jnp.where(lane == idxs[j], prefix, 0.0), axis=-1, keepdims=True)
        rec = jnp.where(lane == float(ROUTE_GATE + j), es[j] / denom, rec)
        rec = jnp.where(lane == float(ROUTE_IDX + j), idxs[j], rec)
        rec = jnp.where(lane == float(ROUTE_RANK + j), rank, rec)
    return rec, cnt + jnp.sum(onehot, axis=0, keepdims=True)


def _post_kernel(*refs, n_parts):
    m_refs, w_refs = refs[:n_parts], refs[n_parts:2 * n_parts]
    (x_ref, g_ref, nw_ref, sh_ref, sc_ref, rw_ref, rb_ref, cnt0_ref,
     xo_ref, h_ref, rt_ref, cnt_ref, cnt) = refs[2 * n_parts:]

    @pl.when(pl.program_id(0) == 0)
    def _():
        cnt[...] = cnt0_ref[...]

    o = _dot(m_refs[0][...], w_refs[0][...])
    for m_ref, w_ref in zip(m_refs[1:], w_refs[1:]):
        o = o + _dot(m_ref[...], w_ref[...])
    xn = x_ref[...] + g_ref[...] * o
    xo_ref[...] = xn
    h = _modulate(xn, nw_ref[...], sh_ref[...], sc_ref[...])
    h_ref[...] = h
    lg = _dot(h.astype(BF16), rw_ref[...]) + rb_ref[...]
    rec, cnt_new = _route_tile(lg, cnt[...])
    rt_ref[...] = rec
    cnt[...] = cnt_new
    cnt_ref[...] = cnt_new


def _post(parts, w_parts, x, nw2, mod, kind, rw, rb, cnt0, tiles_per_seq, name):
    t, d = x.shape
    tm = min(ROW_TILE, t)
    n_parts = len(parts)
    row = lambda n: pl.BlockSpec((tm, n), lambda i: (i, 0))
    full = lambda a: pl.BlockSpec(a.shape, lambda i: (0, 0))
    g_spec, sh_spec, sc_spec = _mod_specs(kind, (2, 3, 4), tm, tiles_per_seq, d)
    return pl.pallas_call(
        functools.partial(_post_kernel, n_parts=n_parts),
        grid=(t // tm,),
        in_specs=[row(p.shape[1]) for p in parts] + [full(w) for w in w_parts]
        + [row(d), g_spec, full(nw2), sh_spec, sc_spec, full(rw), full(rb), full(cnt0)],
        out_specs=[row(d), row(d), row(LANES), full(cnt0)],
        out_shape=[jax.ShapeDtypeStruct((t, d), F32), jax.ShapeDtypeStruct((t, d), F32),
                   jax.ShapeDtypeStruct((t, LANES), F32), jax.ShapeDtypeStruct(cnt0.shape, F32)],
        scratch_shapes=[pltpu.VMEM(cnt0.shape, F32)],
        compiler_params=_cparams(("arbitrary",)),
        name=name,
    )(*parts, *w_parts, x, mod, nw2, mod, mod, rw, rb, cnt0)


def _sb_prompt_kernel(bias_ref, q_ref, k_ref, v_ref, o_ref, acc_ref, run_ref):
    qi = pl.program_id(1)
    upper = _tri(CHUNK, strict=True, lower=True).astype(BF16)
    upper2 = jnp.concatenate([upper, upper], axis=0)
    causal = _tri(CHUNK, strict=True, lower=True)
    heads = range(SB_HEADS)
    cols = [slice(h * SB_HEAD_DIM, (h + 1) * SB_HEAD_DIM) for h in heads]

    def key_block(s0, diagonal):
        zs = [_dot_nt(q_ref[:, cols[h]], k_ref[pl.ds(s0, CHUNK), cols[h]]) for h in heads]
        ls, lks, splits = [], [], []
        for h in heads:
            z = zs[h] * (SB_HEAD_DIM ** -0.5) + bias_ref[h]
            sp = _softplus_sum(z)
            lk = -sp
            if diagonal:
                lk = jnp.where(causal, lk, 0.0)
            ls.append(z - sp)
            lks.append(lk[:, 0:1])
            splits.append(jnp.concatenate(_split_bf16(lk), axis=1))
        betweens = [_dot(splits[h], upper2) for h in heads]
        ws = []
        for h in heads:
            between = betweens[h] if diagonal else betweens[h] + run_ref[h]
            w = jnp.exp(ls[h] + between)
            if diagonal:
                w = jnp.where(causal, w, 0.0)
            ws.append(w.astype(BF16))
            run_ref[h] = between[:, 0:1] + lks[h]
        outs = [_dot(ws[h], v_ref[pl.ds(s0, CHUNK), cols[h]]) for h in heads]
        for h in heads:
            if diagonal:
                acc_ref[:, cols[h]] = outs[h]
            else:
                acc_ref[:, cols[h]] += outs[h]

    key_block(pl.multiple_of(qi * CHUNK, CHUNK), True)

    def body(t, carry):
        key_block(pl.multiple_of((qi - t) * CHUNK, CHUNK), False)
        return carry

    lax.fori_loop(1, qi + 1, body, 0)
    o_ref[...] = acc_ref[...].astype(o_ref.dtype)


def _sb_prompt(q, k, v, bias, batch, seq):
    nq = seq // CHUNK
    return pl.pallas_call(
        _sb_prompt_kernel,
        grid=(batch, nq),
        in_specs=[pl.BlockSpec(memory_space=pltpu.SMEM),
                  pl.BlockSpec((CHUNK, SB_WIDTH), lambda b, i: (b * nq + i, 0)),
                  pl.BlockSpec((seq, SB_WIDTH), lambda b, i: (b, 0)),
                  pl.BlockSpec((seq, SB_WIDTH), lambda b, i: (b, 0))],
        out_specs=pl.BlockSpec((CHUNK, SB_WIDTH), lambda b, i: (b * nq + i, 0)),
        out_shape=jax.ShapeDtypeStruct((batch * seq, SB_WIDTH), BF16),
        scratch_shapes=[pltpu.VMEM((CHUNK, SB_WIDTH), F32), pltpu.VMEM((SB_HEADS, CHUNK, 1), F32)],
        compiler_params=_cparams(("arbitrary", "arbitrary")),
        name="sb_prompt",
    )(bias, q, k, v)


def _sb_sample_kernel(pt_ref, q_ref, bias_ref, *refs, n_pages):
    k_refs, v_refs, o_ref = refs[:n_pages], refs[n_pages:2 * n_pages], refs[2 * n_pages]
    row = lax.broadcasted_iota(jnp.int32, (SB_HEADS, SB_WIDTH), 0)
    col = lax.broadcasted_iota(jnp.int32, (SB_HEADS, SB_WIDTH), 1)
    own = (col // SB_HEAD_DIM) == row
    qm = jnp.where(own, jnp.broadcast_to(q_ref[...].astype(F32), (SB_HEADS, SB_WIDTH)), 0.0).astype(BF16)
    upper = _tri(CHUNK, strict=True, lower=True).astype(BF16)
    upper2 = jnp.concatenate([upper, upper], axis=0)
    bias = bias_ref[...]
    pages = range(n_pages)
    zs = [_dot_nt(qm, k_refs[p][...].astype(BF16)) * (SB_HEAD_DIM ** -0.5) + bias for p in pages]
    sps = [_softplus_sum(z) for z in zs]
    splits = [jnp.concatenate(_split_bf16(-sp), axis=1) for sp in sps]
    betweens = [_dot(s, upper2) for s in splits]
    run = jnp.zeros((SB_HEADS, 1), F32)
    acc = jnp.zeros((SB_HEADS, SB_WIDTH), F32)
    for p in reversed(pages):
        between = betweens[p] + run
        w = jnp.exp((zs[p] - sps[p]) + between)
        acc = acc + _dot(w.astype(BF16), v_refs[p][...].astype(BF16))
        run = between[:, 0:1] - sps[p][:, 0:1]
    o_ref[...] = jnp.sum(jnp.where(own, acc, 0.0), axis=0, keepdims=True).astype(o_ref.dtype)


def _sb_sample(q, ck, cv, layer, page_table, bias):
    bs, n_pages = page_table.shape
    page = ck.shape[2]
    page_spec = lambda p: pl.BlockSpec((None, None, page, SB_WIDTH),
                                       lambda b, pt, p=p: (layer, pt[b * n_pages + p], 0, 0))
    grid_spec = pltpu.PrefetchScalarGridSpec(
        num_scalar_prefetch=1,
        grid=(bs,),
        in_specs=[pl.BlockSpec((None, 1, SB_WIDTH), lambda b, pt: (b, 0, 0)),
                  pl.BlockSpec((SB_HEADS, 1), lambda b, pt: (0, 0))]
        + [page_spec(p) for p in range(n_pages)] * 2,
        out_specs=pl.BlockSpec((None, 1, SB_WIDTH), lambda b, pt: (b, 0, 0)),
    )
    out = pl.pallas_call(
        functools.partial(_sb_sample_kernel, n_pages=n_pages),
        grid_spec=grid_spec,
        out_shape=jax.ShapeDtypeStruct((bs, 1, SB_WIDTH), BF16),
        compiler_params=_cparams(("arbitrary",)),
        name="sb_sample",
    )(page_table.reshape(-1), q.reshape(bs, 1, SB_WIDTH), bias.reshape(SB_HEADS, 1),
      *([ck] * n_pages), *([cv] * n_pages))
    return out.reshape(bs, SB_WIDTH)


def _log_gamma(h):
    return math.log1p(-(2.0 ** (-5.0 - h)))


def _rope_tables(pos):
    half = RET_DIM // 2
    inv = ROPE_BASE ** (-jnp.arange(half, dtype=F32) / half)
    ang = pos.astype(F32)[:, None] * inv[None, :]
    cos, sin = jnp.cos(ang), jnp.sin(ang)
    return jnp.concatenate([cos, cos], axis=-1), jnp.concatenate([-sin, sin], axis=-1)


def _rope(x, cos2, sin2):
    return x * cos2 + pltpu.roll(x, RET_DIM // 2, 1) * sin2


def _group_norm_gate(o, gn, g):
    mu = jnp.mean(o, axis=-1, keepdims=True)
    c = o - mu
    var = jnp.mean(c * c, axis=-1, keepdims=True)
    return _silu(g) * ((c * lax.rsqrt(var + EPS)) * gn)


def _ret_prompt_kernel(q_ref, k_ref, v_ref, g_ref, cos_ref, sin_ref, gn_ref, o_ref, r_ref, state):
    c = pl.program_id(1)

    @pl.when(c == 0)
    def _():
        state[...] = jnp.zeros_like(state)

    cos2, sin2 = cos_ref[...], sin_ref[...]
    li = lax.broadcasted_iota(jnp.int32, (CHUNK, CHUNK), 0)
    mi = lax.broadcasted_iota(jnp.int32, (CHUNK, CHUNK), 1)
    diff = li - mi
    idx = lax.broadcasted_iota(jnp.int32, (CHUNK, 1), 0).astype(F32)
    heads = range(RET_HEADS)
    lgs = [_log_gamma(h) for h in heads]
    cols = [slice(h * RET_DIM, (h + 1) * RET_DIM) for h in heads]
    qbs = [_rope(q_ref[:, cols[h]], cos2, sin2).astype(BF16) for h in heads]
    ks = [_rope(k_ref[:, cols[h]], cos2, sin2) * (RET_DIM ** -0.5) for h in heads]
    vs = [v_ref[:, cols[h]].astype(BF16) for h in heads]
    r_olds = [state[h] for h in heads]
    raw = [_dot_nt(qbs[h], ks[h].astype(BF16)) for h in heads]
    carried = [_dot(qbs[h], r_olds[h].astype(BF16)) for h in heads]
    grown = [_dot_tn((ks[h] * jnp.exp((CHUNK - 1.0 - idx) * lgs[h])).astype(BF16), vs[h]) for h in heads]
    scores = [raw[h] * jnp.where(diff >= 0, jnp.exp(jnp.maximum(diff, 0).astype(F32) * lgs[h]), 0.0) for h in heads]
    inner = [_dot(scores[h].astype(BF16), vs[h]) for h in heads]
    for h in heads:
        state[h] = math.exp(CHUNK * lgs[h]) * r_olds[h] + grown[h]
        from_state = carried[h] * jnp.exp((idx + 1.0) * lgs[h])
        o = _group_norm_gate(inner[h] + from_state, gn_ref[:, cols[h]], g_ref[:, cols[h]])
        o_ref[:, cols[h]] = o.astype(o_ref.dtype)

    @pl.when(c == pl.num_programs(1) - 1)
    def _():
        r_ref[...] = state[...]


def _ret_prompt(ret, gn_w, batch, seq):
    nc = seq // CHUNK
    cos2, sin2 = _rope_tables(jnp.arange(seq, dtype=jnp.int32))
    col = lambda j: pl.BlockSpec((CHUNK, RET_WIDTH), lambda b, c, j=j: (b * nc + c, j))
    tab = pl.BlockSpec((CHUNK, RET_DIM), lambda b, c: (c, 0))
    return pl.pallas_call(
        _ret_prompt_kernel,
        grid=(batch, nc),
        in_specs=[col(0), col(1), col(2), col(3), tab, tab, pl.BlockSpec((1, RET_WIDTH), lambda b, c: (0, 0))],
        out_specs=[pl.BlockSpec((CHUNK, RET_WIDTH), lambda b, c: (b * nc + c, 0)),
                   pl.BlockSpec((None, RET_HEADS, RET_DIM, RET_DIM), lambda b, c: (b, 0, 0, 0))],
        out_shape=[jax.ShapeDtypeStruct((batch * seq, RET_WIDTH), BF16),
                   jax.ShapeDtypeStruct((batch, RET_HEADS, RET_DIM, RET_DIM), F32)],
        scratch_shapes=[pltpu.VMEM((RET_HEADS, RET_DIM, RET_DIM), F32)],
        compiler_params=_cparams(("arbitrary", "arbitrary")),
        name="ret_prompt",
    )(ret, ret, ret, ret, cos2, sin2, gn_w)


RET_SEQS = 8


def _ret_sample_kernel(q_ref, k_ref, v_ref, g_ref, cos_ref, sin_ref, gn_ref, r0_ref, o_ref, r_ref):
    cos2, sin2 = cos_ref[...], sin_ref[...]
    for h in range(RET_HEADS):
        gamma = math.exp(_log_gamma(h))
        cols = slice(h * RET_DIM, (h + 1) * RET_DIM)
        q = _rope(q_ref[:, cols], cos2, sin2)
        k = _rope(k_ref[:, cols], cos2, sin2) * (RET_DIM ** -0.5)
        v = v_ref[:, cols]
        qb, kb, vb = q.astype(BF16), k.astype(BF16), v.astype(BF16)
        qk = jnp.sum(qb.astype(F32) * kb.astype(F32), axis=-1, keepdims=True)
        inner = qk * vb.astype(F32)
        rows = []
        for s in range(RET_SEQS):
            r_old = r0_ref[s, h]
            rows.append(_dot(qb[s:s + 1, :], r_old.astype(BF16)) * gamma)
            k_col = jnp.transpose(kb[s:s + 1, :].astype(F32))
            r_ref[s, h] = gamma * r_old + k_col * vb[s:s + 1, :].astype(F32)
        o = inner + jnp.concatenate(rows, axis=0)
        o_ref[:, cols] = _group_norm_gate(o, gn_ref[:, cols], g_ref[:, cols]).astype(o_ref.dtype)


def _ret_sample(ret, gn_w, r0, layer, past_len):
    bs = ret.shape[0]
    cos2, sin2 = _rope_tables(jnp.full((1,), past_len, jnp.int32))
    col = lambda j: pl.BlockSpec((RET_SEQS, RET_WIDTH), lambda i, j=j: (i, j))
    one = lambda n: pl.BlockSpec((1, n), lambda i: (0, 0))
    st_in = pl.BlockSpec((None, RET_SEQS, RET_HEADS, RET_DIM, RET_DIM), lambda i: (layer, i, 0, 0, 0))
    st = pl.BlockSpec((RET_SEQS, RET_HEADS, RET_DIM, RET_DIM), lambda i: (i, 0, 0, 0))
    return pl.pallas_call(
        _ret_sample_kernel,
        grid=(bs // RET_SEQS,),
        in_specs=[col(0), col(1), col(2), col(3), one(RET_DIM), one(RET_DIM), one(RET_WIDTH), st_in],
        out_specs=[pl.BlockSpec((RET_SEQS, RET_WIDTH), lambda i: (i, 0)), st],
        out_shape=[jax.ShapeDtypeStruct((bs, RET_WIDTH), BF16),
                   jax.ShapeDtypeStruct(r0.shape[1:], F32)],
        compiler_params=_cparams(("arbitrary",)),
        name="ret_sample",
    )(ret, ret, ret, ret, cos2, sin2, gn_w, r0)


def _gated_group_rms(y, z, nw, groups):
    yg = y * _silu(z)
    gw = y.shape[-1] // groups
    outs = []
    for g in range(groups):
        a = yg[:, g * gw:(g + 1) * gw]
        outs.append(a * lax.rsqrt(jnp.mean(a * a, axis=-1, keepdims=True) + EPS))
    return jnp.concatenate(outs, axis=-1) * nw


def _ssd_prompt_kernel(z_ref, xbc_ref, dt_ref, cw_ref, cb_ref, dtb_ref, alog_ref, dsk_ref, nw_ref, sel_ref,
                       y_ref, s_ref, cv_ref, state, win, ybuf, *, heads, width):
    c = pl.program_id(1)
    hpg = heads // SSD_GROUPS
    gn = SSD_GROUPS * SSD_STATE

    @pl.when(c == 0)
    def _():
        state[...] = jnp.zeros_like(state)
        win[0:8, :] = jnp.zeros((8, win.shape[1]), F32)

    win[8:8 + CHUNK, :] = xbc_ref[...]
    conv = cb_ref[...]
    for j in range(SSD_CONV):
        conv = conv + win[5 + j:5 + j + CHUNK, :] * cw_ref[j:j + 1, :]
    act = _silu(conv)

    @pl.when(c == pl.num_programs(1) - 1)
    def _():
        cv_ref[...] = win[CHUNK + 5:CHUNK + 8, :]

    win[0:8, :] = win[CHUNK:CHUNK + 8, :]

    dt = _softplus(dt_ref[...] + dtb_ref[...])
    a = -jnp.exp(alog_ref[...])
    da_hi, da_lo = _split_bf16(dt * a)
    lower = _tri(CHUNK, strict=False, lower=True).astype(BF16)
    acs = _dot(lower, da_hi) + _dot(lower, da_lo)
    acs_t = jnp.transpose(acs)
    dt_t = jnp.transpose(dt)
    both = jnp.concatenate([acs, dt], axis=0)
    p0 = both.astype(BF16)
    r0 = both - p0.astype(F32)
    p1 = r0.astype(BF16)
    p2 = (r0 - p1.astype(F32)).astype(BF16)
    wide = _dot(jnp.concatenate([p0, p1, p2], axis=1), sel_ref[...])
    causal = _tri(CHUNK, strict=False, lower=True)
    p = SSD_HEAD_DIM
    for g in range(SSD_GROUPS):
        bg = act[:, width + g * SSD_STATE:width + (g + 1) * SSD_STATE].astype(BF16)
        cg = act[:, width + gn + g * SSD_STATE:width + gn + (g + 1) * SSD_STATE].astype(BF16)
        cbm = _dot_nt(cg, bg)
        for r in range(hpg):
            h = g * hpg + r
            cols = slice(h * p, (h + 1) * p)
            xh = act[:, cols]
            acs_b = wide[:CHUNK, h * LANES:(h + 1) * LANES]
            dt_b = wide[CHUNK:, h * LANES:h * LANES + p]
            seg = jnp.where(causal, acs_b - acs_t[h:h + 1, :], -jnp.inf)
            w = cbm * jnp.exp(seg) * dt_t[h:h + 1, :]
            s_old = state[h]
            y = _dot(w.astype(BF16), xh.astype(BF16))
            y = y + _dot_nt(cg, s_old.astype(BF16)) * jnp.exp(acs_b[:, :p])
            last = acs_b[CHUNK - 1:CHUNK, :]
            to_end = jnp.exp(last[:, :p] - acs_b[:, :p]) * dt_b
            state[h] = jnp.exp(last) * s_old + _dot_tn((xh * to_end).astype(BF16), bg)
            ybuf[:, cols] = y + dsk_ref[:, cols] * xh

    y_ref[...] = _gated_group_rms(ybuf[...], z_ref[...], nw_ref[...], SSD_GROUPS).astype(y_ref.dtype)

    @pl.when(c == pl.num_programs(1) - 1)
    def _():
        s_ref[...] = state[...]


def _ssd_prompt(z, xbc, dt, cw, cb, dtb, alog, dsk, nw, batch, seq, heads):
    nc = seq // CHUNK
    width = z.shape[1]
    cdim = xbc.shape[1]
    row = lambda n: pl.BlockSpec((CHUNK, n), lambda b, c: (b * nc + c, 0))
    full = lambda a: pl.BlockSpec(a.shape, lambda b, c: (0, 0))
    lane = jnp.arange(LANES, dtype=jnp.int32)[:, None]
    tile = jnp.arange(heads * LANES, dtype=jnp.int32)[None, :] // LANES
    sel = jnp.tile((lane == tile).astype(BF16), (3, 1))
    return pl.pallas_call(
        functools.partial(_ssd_prompt_kernel, heads=heads, width=width),
        grid=(batch, nc),
        in_specs=[row(width), row(cdim), row(LANES), full(cw), full(cb), full(dtb), full(alog), full(dsk), full(nw),
                  full(sel)],
        out_specs=[row(width),
                   pl.BlockSpec((None, heads, SSD_HEAD_DIM, SSD_STATE), lambda b, c: (b, 0, 0, 0)),
                   pl.BlockSpec((None, SSD_CONV - 1, cdim), lambda b, c: (b, 0, 0))],
        out_shape=[jax.ShapeDtypeStruct((batch * seq, width), BF16),
                   jax.ShapeDtypeStruct((batch, heads, SSD_HEAD_DIM, SSD_STATE), F32),
                   jax.ShapeDtypeStruct((batch, SSD_CONV - 1, cdim), F32)],
        scratch_shapes=[pltpu.VMEM((heads, SSD_HEAD_DIM, SSD_STATE), F32),
                        pltpu.VMEM((CHUNK + 8, cdim), F32),
                        pltpu.VMEM((CHUNK, width), F32)],
        compiler_params=_cparams(("arbitrary", "arbitrary")),
        name="ssd_prompt",
    )(z, xbc, dt, cw, cb, dtb, alog, dsk, nw, sel)


SSD_SEQS = 4


def _ssd_sample_kernel(z_ref, xbc_ref, dt_ref, cprev_ref, cw_ref, cb_ref, dtb_ref, alog_ref, dsk_ref, nw_ref,
                       s0_ref, y_ref, s_ref, cv_ref, *, heads, width):
    hpg = heads // SSD_GROUPS
    gn = SSD_GROUPS * SSD_STATE
    for s in range(SSD_SEQS):
        dt = _softplus(dt_ref[s] + dtb_ref[...])
        decay = jnp.exp(dt * (-jnp.exp(alog_ref[...])))
        xrow = xbc_ref[s]
        prev = cprev_ref[s]
        conv = cb_ref[...] + xrow * cw_ref[SSD_CONV - 1:SSD_CONV, :]
        for j in range(SSD_CONV - 1):
            conv = conv + prev[j:j + 1, :] * cw_ref[j:j + 1, :]
        cv_ref[s, 0:SSD_CONV - 2, :] = prev[1:SSD_CONV - 1, :]
        cv_ref[s, SSD_CONV - 2:SSD_CONV - 1, :] = xrow
        act = _silu(conv)
        yrow = []
        for g in range(SSD_GROUPS):
            bg = act[:, width + g * SSD_STATE:width + (g + 1) * SSD_STATE].astype(BF16).astype(F32)
            cg = act[:, width + gn + g * SSD_STATE:width + gn + (g + 1) * SSD_STATE].astype(BF16)
            for r in range(hpg):
                h = g * hpg + r
                cols = slice(h * SSD_HEAD_DIM, (h + 1) * SSD_HEAD_DIM)
                xh = act[:, cols]
                dtx = (xh * dt[:, h:h + 1]).astype(BF16).astype(F32)
                s_new = decay[:, h:h + 1] * s0_ref[s, h] + jnp.transpose(dtx) * bg
                s_ref[s, h] = s_new
                yrow.append(_dot_nt(cg, s_new.astype(BF16)) + dsk_ref[:, cols] * xh)
        y = jnp.concatenate(yrow, axis=-1)
        y_ref[s] = _gated_group_rms(y, z_ref[s], nw_ref[...], SSD_GROUPS).astype(y_ref.dtype)


def _ssd_sample(z, xbc, dt, conv_prev, s0, layer, cw, cb, dtb, alog, dsk, nw, heads):
    bs, width = z.shape
    cdim = xbc.shape[1]
    row = lambda n: pl.BlockSpec((SSD_SEQS, 1, n), lambda i: (i, 0, 0))
    full = lambda a: pl.BlockSpec(a.shape, lambda i: (0, 0))
    st_in = pl.BlockSpec((None, SSD_SEQS, heads, SSD_HEAD_DIM, SSD_STATE), lambda i: (layer, i, 0, 0, 0))
    cv_in = pl.BlockSpec((None, SSD_SEQS, SSD_CONV - 1, cdim), lambda i: (layer, i, 0, 0))
    st = pl.BlockSpec((SSD_SEQS, heads, SSD_HEAD_DIM, SSD_STATE), lambda i: (i, 0, 0, 0))
    cv = pl.BlockSpec((SSD_SEQS, SSD_CONV - 1, cdim), lambda i: (i, 0, 0))
    y, s_new, cv_new = pl.pallas_call(
        functools.partial(_ssd_sample_kernel, heads=heads, width=width),
        grid=(bs // SSD_SEQS,),
        in_specs=[row(width), row(cdim), row(LANES), cv_in, full(cw), full(cb), full(dtb), full(alog), full(dsk),
                  full(nw), st_in],
        out_specs=[row(width), st, cv],
        out_shape=[jax.ShapeDtypeStruct((bs, 1, width), BF16),
                   jax.ShapeDtypeStruct(s0.shape[1:], F32),
                   jax.ShapeDtypeStruct(conv_prev.shape[1:], F32)],
        compiler_params=_cparams(("arbitrary",)),
        name="ssd_sample",
    )(z.reshape(bs, 1, width), xbc.reshape(bs, 1, cdim), dt.reshape(bs, 1, LANES), conv_prev, cw, cb, dtb, alog,
      dsk, nw, s0)
    return y.reshape(bs, width), s_new, cv_new


def _expert_kernel(be_ref, nu_ref, x_ref, w1_ref, b1_ref, w2_ref, b2_ref, o_ref, w1s, w2s, *, f):
    i = pl.program_id(0)
    prev = be_ref[jnp.maximum(i - 1, 0)]

    @pl.when((i == 0) | (be_ref[i] != prev))
    def _():
        w1s[...] = w1_ref[...].astype(BF16)
        w2s[...] = w2_ref[...].astype(BF16)

    @pl.when(i < nu_ref[0])
    def _():
        x = x_ref[...].astype(BF16)
        acc = jnp.zeros(o_ref.shape, F32) + b2_ref[...]
        half = f // 2
        for c in range(2):
            glu = _dot(x, w1s[:, c * half:(c + 1) * half]) + b1_ref[:, c * half:(c + 1) * half]
            lin = _dot(x, w1s[:, f + c * half:f + (c + 1) * half]) + b1_ref[:, f + c * half:f + (c + 1) * half]
            glu = jnp.minimum(glu, SWIGLU_LIMIT)
            lin = jnp.clip(lin, -SWIGLU_LIMIT, SWIGLU_LIMIT)
            a = glu * jax.nn.sigmoid(SWIGLU_ALPHA * glu) * (lin + 1.0)
            acc = acc + _dot(a.astype(BF16), w2s[c * half:(c + 1) * half, :])
        o_ref[...] = acc

    @pl.when(i >= nu_ref[0])
    def _():
        o_ref[...] = jnp.zeros_like(o_ref)


def _experts(block_expert, n_used, xb, layer, w1, b1, w2, b2):
    n_rows = xb.shape[0]
    _, n_exp, d, f2 = w1.shape
    f = f2 // 2
    n_blocks = n_rows // MOE_ROWS
    rows = pl.BlockSpec((MOE_ROWS, d), lambda i, be, nu: (i, 0))
    grid_spec = pltpu.PrefetchScalarGridSpec(
        num_scalar_prefetch=2,
        grid=(n_blocks,),
        in_specs=[rows,
                  pl.BlockSpec((None, None, d, f2), lambda i, be, nu: (layer, be[i], 0, 0)),
                  pl.BlockSpec((None, None, 1, f2), lambda i, be, nu: (layer, be[i], 0, 0)),
                  pl.BlockSpec((None, None, f, d), lambda i, be, nu: (layer, be[i], 0, 0)),
                  pl.BlockSpec((None, None, 1, d), lambda i, be, nu: (layer, be[i], 0, 0))],
        out_specs=rows,
        scratch_shapes=[pltpu.VMEM((d, f2), BF16), pltpu.VMEM((f, d), BF16)],
    )
    depth = w1.shape[0]
    return pl.pallas_call(
        functools.partial(_expert_kernel, f=f),
        grid_spec=grid_spec,
        out_shape=jax.ShapeDtypeStruct(xb.shape, F32),
        compiler_params=_cparams(("arbitrary",)),
        name="moe_experts",
    )(block_expert, n_used, xb, w1, b1.reshape(depth, n_exp, 1, f2), w2, b2.reshape(depth, n_exp, 1, d))


def _expert_layout(counts, n_blocks):
    n_exp = counts.shape[0]
    padded = (counts + MOE_ROWS - 1) // MOE_ROWS * MOE_ROWS
    pad_end = jnp.cumsum(padded)
    starts = jnp.arange(n_blocks, dtype=jnp.int32) * MOE_ROWS
    block_expert = jnp.minimum(jnp.sum(starts[:, None] >= pad_end[None, :], axis=1), n_exp - 1).astype(jnp.int32)
    return pad_end - padded, block_expert, (pad_end[-1] // MOE_ROWS).astype(jnp.int32).reshape(1)


def _row_positions(rec, pad_start):
    n_exp = pad_start.shape[0]
    idx = rec[:, ROUTE_IDX:ROUTE_IDX + TOP_K].astype(jnp.int32)
    rank = rec[:, ROUTE_RANK:ROUTE_RANK + TOP_K].astype(jnp.int32)
    start = jnp.sum(jnp.where(idx[..., None] == jnp.arange(n_exp, dtype=jnp.int32), pad_start, 0), axis=-1)
    return (start + rank).reshape(-1).astype(jnp.int32)


SUBLANES = 8
COMBINE_ROWS = 32


def _dispatch_kernel(pos_ref, h_ref, xb_in, xb_ref, sem):
    del xb_in
    tm = h_ref.shape[0]
    base = pl.program_id(0) * (tm * TOP_K)

    def row_copy(g, k, j):
        t0 = pl.multiple_of(g * SUBLANES, SUBLANES)
        dst = pos_ref[base + g * (SUBLANES * TOP_K) + (k * TOP_K + j)]
        return pltpu.make_async_copy(h_ref.at[pl.ds(t0 + k, 1), :], xb_ref.at[pl.ds(dst, 1), :], sem)

    def issue(g, carry):
        for k in range(SUBLANES):
            for j in range(TOP_K):
                row_copy(g, k, j).start()
        return carry

    def drain(g, carry):
        for k in range(SUBLANES):
            for j in range(TOP_K):
                row_copy(g, k, j).wait()
        return carry

    lax.fori_loop(0, tm // SUBLANES, issue, 0)
    lax.fori_loop(0, tm // SUBLANES, drain, 0)


def _dispatch(pos, h, xb, name):
    t, d = h.shape
    tm = min(ROW_TILE, t)
    grid_spec = pltpu.PrefetchScalarGridSpec(
        num_scalar_prefetch=1,
        grid=(t // tm,),
        in_specs=[pl.BlockSpec((tm, d), lambda i, p: (i, 0)),
                  pl.BlockSpec(memory_space=pl.ANY)],
        out_specs=pl.BlockSpec(memory_space=pl.ANY),
        scratch_shapes=[pltpu.SemaphoreType.DMA],
    )
    return pl.pallas_call(
        _dispatch_kernel,
        grid_spec=grid_spec,
        out_shape=jax.ShapeDtypeStruct(xb.shape, xb.dtype),
        input_output_aliases={2: 0},
        compiler_params=_cparams(("arbitrary",)),
        name=name,
    )(pos, h, xb)


def _combine_kernel(pos_ref, x_ref, g_ref, rec_ref, yb_ref, *rest, final):
    if final:
        fw_ref, o_ref, ybuf, sem = rest
    else:
        o_ref, ybuf, sem = rest
    tm = x_ref.shape[0]
    base = pl.program_id(0) * (tm * TOP_K)

    def row_copy(g, k, j):
        t0 = pl.multiple_of(g * SUBLANES, SUBLANES)
        src = pos_ref[base + g * (SUBLANES * TOP_K) + (k * TOP_K + j)]
        return pltpu.make_async_copy(yb_ref.at[pl.ds(src, 1), :], ybuf.at[j, pl.ds(t0 + k, 1), :], sem)

    def issue(g, carry):
        for k in range(SUBLANES):
            for j in range(TOP_K):
                row_copy(g, k, j).start()
        return carry

    def drain(g, carry):
        for k in range(SUBLANES):
            for j in range(TOP_K):
                row_copy(g, k, j).wait()
        return carry

    lax.fori_loop(0, tm // SUBLANES, issue, 0)
    lax.fori_loop(0, tm // SUBLANES, drain, 0)

    def rows(c, carry):
        r = pl.ds(pl.multiple_of(c * COMBINE_ROWS, COMBINE_ROWS), COMBINE_ROWS)
        rec = rec_ref[r, :]
        y = rec[:, ROUTE_GATE:ROUTE_GATE + 1] * ybuf[0, r, :]
        for j in range(1, TOP_K):
            y = y + rec[:, ROUTE_GATE + j:ROUTE_GATE + j + 1] * ybuf[j, r, :]
        g = g_ref[...] if g_ref.shape[0] == 1 else g_ref[r, :]
        xn = x_ref[r, :] + g * y
        if final:
            xn = (xn * lax.rsqrt(jnp.mean(xn * xn, axis=-1, keepdims=True) + EPS)) * fw_ref[...]
        o_ref[r, :] = xn
        return carry

    lax.fori_loop(0, tm // COMBINE_ROWS, rows, 0)


def _combine(pos, x, mod, kind, rec, yb, tiles_per_seq, final_w, name):
    t, d = x.shape
    tm = min(ROW_TILE, t)
    (g_spec,) = _mod_specs(kind, (5,), tm, tiles_per_seq, d)
    with_p = lambda spec: pl.BlockSpec(spec.block_shape, lambda i, p, f=spec.index_map: f(i))
    in_specs = [pl.BlockSpec((tm, d), lambda i, p: (i, 0)), with_p(g_spec),
                pl.BlockSpec((tm, LANES), lambda i, p: (i, 0)),
                pl.BlockSpec(memory_space=pl.ANY)]
    args = [pos, x, mod, rec, yb]
    if final_w is not None:
        in_specs.append(pl.BlockSpec((1, d), lambda i, p: (0, 0)))
        args.append(final_w)
    grid_spec = pltpu.PrefetchScalarGridSpec(
        num_scalar_prefetch=1,
        grid=(t // tm,),
        in_specs=in_specs,
        out_specs=pl.BlockSpec((tm, d), lambda i, p: (i, 0)),
        scratch_shapes=[pltpu.VMEM((TOP_K, tm, d), F32), pltpu.SemaphoreType.DMA],
    )
    return pl.pallas_call(
        functools.partial(_combine_kernel, final=final_w is not None),
        grid_spec=grid_spec,
        out_shape=jax.ShapeDtypeStruct((t, d), F32),
        compiler_params=_cparams(("arbitrary",)),
        name=name,
    )(*args)


def kernel(x_prompt, x_sample, cache_sb_k, cache_sb_v, page_table, state_ret, state_ssd, state_conv,
           c_prompt, c_sample, norm1_w, norm2_w, w_ada, b_ada, ab_w_in, ab_w_out, ret_gn_w, sb_bias,
           ssd_w_in, ssd_conv_w, ssd_conv_b, ssd_dt_bias, ssd_a_log, ssd_d, ssd_norm_w, ssd_w_out,
           router_w, router_b, moe_w1, moe_b1, moe_w2, moe_b2, final_norm_w):
    bp, sp, d = x_prompt.shape
    bs, ls, _ = x_sample.shape
    assert ls == 1 and sp % CHUNK == 0 and cache_sb_k.shape[2] == CHUNK
    depth = w_ada.shape[0]
    n_exp = router_w.shape[-1]
    tp = bp * sp
    past_len = page_table.shape[1] * cache_sb_k.shape[2]
    ssd_width = ssd_norm_w.shape[-1]
    ssd_heads = ssd_a_log.shape[-1]
    cdim = ssd_conv_w.shape[-1]
    tps = sp // min(ROW_TILE, tp)
    assert d % LANES == 0 and n_exp <= LANES
    n_blocks = -(-((tp + bs) * TOP_K) // MOE_ROWS) + n_exp

    mod = _ada_mod(jnp.concatenate([c_prompt, c_sample], axis=0), w_ada, b_ada)
    xp = x_prompt.reshape(tp, d)
    xs = x_sample.reshape(bs, d)
    row1 = lambda a: a.reshape(1, -1)
    pad_l = lambda a: jnp.pad(a, ((0, 0), (0, LANES - a.shape[-1])))

    sbk_p, sbv_p, sbk_s, sbv_s, ret_p, ret_s = [], [], [], [], [], []
    ssd_p, ssd_s, conv_p, conv_s = [], [], [], []
    y_prompt = y_sample = None
    xb = jnp.zeros((n_blocks * MOE_ROWS, d), F32)
    pool = cache_sb_k.shape[:3] + (SB_WIDTH,)
    ck = cache_sb_k.reshape(pool)
    cv = cache_sb_v.reshape(pool)
    for layer in range(depth):
        i = layer // 2
        modp = mod[layer, :bp].reshape(bp * 6, 1, d)
        mods = mod[layer, bp:]
        nw1 = row1(norm1_w[layer])
        if layer % 2 == 0:
            w_in = ab_w_in[i].astype(BF16)
            ws = [w_in[:, :SB_WIDTH], w_in[:, SB_WIDTH:2 * SB_WIDTH], w_in[:, 2 * SB_WIDTH:3 * SB_WIDTH],
                  w_in[:, 3 * SB_WIDTH:]]
            plan = ((BF16,), (F32, BF16), (F32, BF16), (F32,))
            qa_p, ka_p, kab_p, va_p, vab_p, rt_p = _inproj(xp, nw1, modp, "p", ws, plan, tps, "ab_in_prompt")
            qa_s, ka_s, _, va_s, _, rt_s = _inproj(xs, nw1, mods, "s", ws, plan, 1, "ab_in_sample")
            gn = row1(ret_gn_w[i])
            o_sb_p = _sb_prompt(qa_p, kab_p, vab_p, sb_bias[i], bp, sp)
            o_rt_p, r_p = _ret_prompt(rt_p, gn, bp, sp)
            o_sb_s = _sb_sample(qa_s, ck, cv, i, page_table, sb_bias[i])
            o_rt_s, r_s = _ret_sample(rt_s, gn, state_ret, i, past_len)
            parts_p, parts_s = [o_sb_p, o_rt_p], [o_sb_s, o_rt_s]
            w_out = ab_w_out[i].astype(BF16)
            w_parts = [w_out[:SB_WIDTH], w_out[SB_WIDTH:]]
            sbk_p.append(ka_p.reshape(bp, sp, SB_HEADS, SB_HEAD_DIM))
            sbv_p.append(va_p.reshape(bp, sp, SB_HEADS, SB_HEAD_DIM))
            sbk_s.append(ka_s.reshape(bs, 1, SB_HEADS, SB_HEAD_DIM))
            sbv_s.append(va_s.reshape(bs, 1, SB_HEADS, SB_HEAD_DIM))
            ret_p.append(r_p)
            ret_s.append(r_s)
        else:
            w_in = ssd_w_in[i]
            ws = [w_in[:, :ssd_width].astype(BF16), w_in[:, ssd_width:ssd_width + cdim].astype(BF16),
                  pad_l(w_in[:, ssd_width + cdim:]).astype(BF16)]
            plan = ((F32,), (F32,), (F32,))
            z_p, xbc_p, dt_p = _inproj(xp, nw1, modp, "p", ws, plan, tps, "ssd_in_prompt")
            z_s, xbc_s, dt_s = _inproj(xs, nw1, mods, "s", ws, plan, 1, "ssd_in_sample")
            consts = (ssd_conv_w[i], row1(ssd_conv_b[i]), pad_l(row1(ssd_dt_bias[i])), pad_l(row1(ssd_a_log[i])),
                      row1(jnp.repeat(ssd_d[i], SSD_HEAD_DIM)), row1(ssd_norm_w[i]))
            mixed_p, s_p, cv_p = _ssd_prompt(z_p, xbc_p, dt_p, *consts, bp, sp, ssd_heads)
            mixed_s, s_s, cv_s = _ssd_sample(z_s, xbc_s, dt_s, state_conv, state_ssd, i, *consts, ssd_heads)
            parts_p, parts_s = [mixed_p], [mixed_s]
            w_parts = [ssd_w_out[i].astype(BF16)]
            ssd_p.append(s_p)
            ssd_s.append(s_s)
            conv_p.append(cv_p)
            conv_s.append(cv_s)

        rw = pad_l(router_w[layer]).astype(BF16)
        rb = jnp.pad(row1(router_b[layer]), ((0, 0), (0, LANES - n_exp)), constant_values=ROUTE_PAD_LOGIT)
        nw2 = row1(norm2_w[layer])
        cnt0 = jnp.zeros((1, LANES), F32)
        xp, h2p, rec_p, cnt_p = _post(parts_p, w_parts, xp, nw2, modp, "p", rw, rb, cnt0, tps, "post_prompt")
        xs, h2s, rec_s, cnt = _post(parts_s, w_parts, xs, nw2, mods, "s", rw, rb, cnt_p, 1, "post_sample")

        pad_start, block_expert, n_used = _expert_layout(cnt[0, :n_exp].astype(jnp.int32), n_blocks)
        pos_p = _row_positions(rec_p, pad_start)
        pos_s = _row_positions(rec_s, pad_start)
        xb = _dispatch(pos_p, h2p, xb, "dispatch_prompt")
        xb = _dispatch(pos_s, h2s, xb, "dispatch_sample")
        yb = _experts(block_expert, n_used, xb, layer, moe_w1, moe_b1, moe_w2, moe_b2)
        last = layer == depth - 1
        fw = row1(final_norm_w) if last else None
        xp = _combine(pos_p, xp, modp, "p", rec_p, yb, tps, fw, "combine_prompt")
        xs = _combine(pos_s, xs, mods, "s", rec_s, yb, 1, fw, "combine_sample")

    st = lambda lst: jnp.stack(lst, axis=0)
    return (xp.reshape(bp, sp, d), xs.reshape(bs, ls, d), st(sbk_p), st(sbv_p), st(sbk_s), st(sbv_s),
            st(ret_p), st(ret_s), st(ssd_p), st(ssd_s), st(conv_p), st(conv_s))
```

```python
import functools
import math

import jax
import jax.numpy as jnp
from jax import lax
from jax.experimental import pallas as pl
from jax.experimental.pallas import tpu as pltpu

F32 = jnp.float32
BF16 = jnp.bfloat16

EPS = 1e-6
ROPE_BASE = 10000.0
SB_HEADS = 8
SB_HEAD_DIM = 64
SB_WIDTH = SB_HEADS * SB_HEAD_DIM
RET_HEADS = 4
RET_DIM = 128
RET_WIDTH = RET_HEADS * RET_DIM
SSD_HEAD_DIM = 64
SSD_GROUPS = 4
SSD_STATE = 128
SSD_CONV = 4
TOP_K = 4
SWIGLU_LIMIT = 7.0
SWIGLU_ALPHA = 1.702
CHUNK = 128
LANES = 128
ROW_TILE = 256
MOE_ROWS = 256
VMEM_LIMIT = 56 * 1024 * 1024


def _cparams(sem):
    return pltpu.CompilerParams(dimension_semantics=sem, vmem_limit_bytes=VMEM_LIMIT)


def _silu(x):
    return x * jax.nn.sigmoid(x)


def _softplus(x):
    return jnp.maximum(x, 0.0) + jnp.log1p(jnp.exp(-jnp.abs(x)))


def _softplus_sum(x):
    return jnp.maximum(x, 0.0) + jnp.log(1.0 + jnp.exp(-jnp.abs(x)))


def _modulate(x, nw, shift, scale):
    y = x * lax.rsqrt(jnp.mean(x * x, axis=-1, keepdims=True) + EPS)
    return (y * nw) * (1.0 + scale) + shift


def _split_bf16(x):
    hi = x.astype(BF16)
    lo = (x - hi.astype(F32)).astype(BF16)
    return hi, lo


def _dot(a, b):
    return jnp.dot(a, b, preferred_element_type=F32)


def _dot_nt(a, b):
    return lax.dot_general(a, b, (((1,), (1,)), ((), ())), preferred_element_type=F32)


def _dot_tn(a, b):
    return lax.dot_general(a, b, (((0,), (0,)), ((), ())), preferred_element_type=F32)


def _tri(n, strict, lower):
    r = lax.broadcasted_iota(jnp.int32, (n, n), 0)
    c = lax.broadcasted_iota(jnp.int32, (n, n), 1)
    if lower:
        m = (c < r) if strict else (c <= r)
    else:
        m = (c > r) if strict else (c >= r)
    return m


def _mod_specs(kind, ks, tm, tiles_per_seq, d):
    if kind == "p":
        return [pl.BlockSpec((None, 1, d), lambda i, k=k: ((i // tiles_per_seq) * 6 + k, 0, 0)) for k in ks]
    return [pl.BlockSpec((tm, d), lambda i, k=k: (i, k)) for k in ks]


def _ada_kernel(c_ref, w_ref, b_ref, o_ref):
    c = _silu(c_ref[...]).astype(BF16)
    o_ref[...] = _dot(c, w_ref[...].astype(BF16)) + b_ref[...]


def _ada_mod(c, w_ada, b_ada):
    depth, d, n = w_ada.shape
    rows = c.shape[0]
    tn = 1024
    return pl.pallas_call(
        _ada_kernel,
        grid=(depth, n // tn),
        in_specs=[pl.BlockSpec((rows, d), lambda l, j: (0, 0)),
                  pl.BlockSpec((None, d, tn), lambda l, j: (l, 0, j)),
                  pl.BlockSpec((None, 1, tn), lambda l, j: (l, 0, j))],
        out_specs=pl.BlockSpec((None, rows, tn), lambda l, j: (l, 0, j)),
        out_shape=jax.ShapeDtypeStruct((depth, rows, n), F32),
        compiler_params=_cparams(("arbitrary", "arbitrary")),
        name="ada_mod",
    )(c, w_ada, b_ada.reshape(depth, 1, n))


def _inproj_kernel(x_ref, nw_ref, sh_ref, sc_ref, *refs, plan):
    n_w = len(plan)
    w_refs, o_refs = refs[:n_w], refs[n_w:]
    h = _modulate(x_ref[...], nw_ref[...], sh_ref[...], sc_ref[...]).astype(BF16)
    oi = 0
    for w_ref, dts in zip(w_refs, plan):
        y = _dot(h, w_ref[...])
        for dt in dts:
            o_refs[oi][...] = y.astype(dt)
            oi += 1


def _inproj(x, nw, mod, kind, ws, plan, tiles_per_seq, name):
    t, d = x.shape
    tm = min(ROW_TILE, t)
    out_shape, out_specs = [], []
    for w, dts in zip(ws, plan):
        for dt in dts:
            out_shape.append(jax.ShapeDtypeStruct((t, w.shape[1]), dt))
            out_specs.append(pl.BlockSpec((tm, w.shape[1]), lambda i: (i, 0)))
    return pl.pallas_call(
        functools.partial(_inproj_kernel, plan=plan),
        grid=(t // tm,),
        in_specs=[pl.BlockSpec((tm, d), lambda i: (i, 0)),
                  pl.BlockSpec((1, d), lambda i: (0, 0))]
        + _mod_specs(kind, (0, 1), tm, tiles_per_seq, d)
        + [pl.BlockSpec(w.shape, lambda i: (0, 0)) for w in ws],
        out_specs=out_specs,
        out_shape=out_shape,
        compiler_params=_cparams(("arbitrary",)),
        name=name,
    )(x, nw, mod, mod, *ws)


ROUTE_GATE, ROUTE_IDX, ROUTE_RANK = 0, TOP_K, 2 * TOP_K
ROUTE_PAD_LOGIT = -1e30


def _route_tile(lg, cnt):
    tm = lg.shape[0]
    lane = lax.broadcasted_iota(jnp.int32, lg.shape, 1).astype(F32)
    work = lg
    vals, idxs = [], []
    for _ in range(TOP_K):
        m = jnp.max(work, axis=-1, keepdims=True)
        ix = jnp.min(jnp.where(work == m, lane, float(LANES)), axis=-1, keepdims=True)
        vals.append(m)
        idxs.append(ix)
        work = jnp.where(lane == ix, -jnp.inf, work)
    es = [jnp.exp(v - vals[0]) for v in vals]
    denom = es[0]
    for e in es[1:]:
        denom = denom + e
    onehot = jnp.zeros(lg.shape, F32)
    for ix in idxs:
        onehot = onehot + jnp.where(lane == ix, 1.0, 0.0)
    before = _tri(tm, strict=True, lower=True).astype(BF16)
    prefix = _dot(before, onehot.astype(BF16)) + cnt
    rec = jnp.zeros(lg.shape, F32)
    for j in range(TOP_K):
        rank = jnp.sum(jnp.where(lane == idxs[j], prefix, 0.0), axis=-1, keepdims=True)
        rec = jnp.where(lane == float(ROUTE_GATE + j), es[j] / denom, rec)
        rec = jnp.where(lane == float(ROUTE_IDX + j), idxs[j], rec)
        rec = jnp.where(lane == float(ROUTE_RANK + j), rank, rec)
    return rec, cnt + jnp.sum(onehot, axis=0, keepdims=True)


def _post_kernel(*refs, n_parts):
    m_refs, w_refs = refs[:n_parts], refs[n_parts:2 * n_parts]
    (x_ref, g_ref, nw_ref, sh_ref, sc_ref, rw_ref, rb_ref, cnt0_ref,
     xo_ref, h_ref, rt_ref, cnt_ref, cnt) = refs[2 * n_parts:]

    @pl.when(pl.program_id(0) == 0)
    def _():
        cnt[...] = cnt0_ref[...]

    o = _dot(m_refs[0][...], w_refs[0][...])
    for m_ref, w_ref in zip(m_refs[1:], w_refs[1:]):
        o = o + _dot(m_ref[...], w_ref[...])
    xn = x_ref[...] + g_ref[...] * o
    xo_ref[...] = xn
    h = _modulate(xn, nw_ref[...], sh_ref[...], sc_ref[...])
    h_ref[...] = h
    lg = _dot(h.astype(BF16), rw_ref[...]) + rb_ref[...]
    rec, cnt_new = _route_tile(lg, cnt[...])
    rt_ref[...] = rec
    cnt[...] = cnt_new
    cnt_ref[...] = cnt_new


def _post(parts, w_parts, x, nw2, mod, kind, rw, rb, cnt0, tiles_per_seq, name):
    t, d = x.shape
    tm = min(ROW_TILE, t)
    n_parts = len(parts)
    row = lambda n: pl.BlockSpec((tm, n), lambda i: (i, 0))
    full = lambda a: pl.BlockSpec(a.shape, lambda i: (0, 0))
    g_spec, sh_spec, sc_spec = _mod_specs(kind, (2, 3, 4), tm, tiles_per_seq, d)
    return pl.pallas_call(
        functools.partial(_post_kernel, n_parts=n_parts),
        grid=(t // tm,),
        in_specs=[row(p.shape[1]) for p in parts] + [full(w) for w in w_parts]
        + [row(d), g_spec, full(nw2), sh_spec, sc_spec, full(rw), full(rb), full(cnt0)],
        out_specs=[row(d), row(d), row(LANES), full(cnt0)],
        out_shape=[jax.ShapeDtypeStruct((t, d), F32), jax.ShapeDtypeStruct((t, d), F32),
                   jax.ShapeDtypeStruct((t, LANES), F32), jax.ShapeDtypeStruct(cnt0.shape, F32)],
        scratch_shapes=[pltpu.VMEM(cnt0.shape, F32)],
        compiler_params=_cparams(("arbitrary",)),
        name=name,
    )(*parts, *w_parts, x, mod, nw2, mod, mod, rw, rb, cnt0)


def _sb_prompt_kernel(bias_ref, q_ref, k_ref, v_ref, o_ref, acc_ref, run_ref):
    qi = pl.program_id(1)
    upper = _tri(CHUNK, strict=True, lower=True).astype(BF16)
    upper2 = jnp.concatenate([upper, upper], axis=0)
    causal = _tri(CHUNK, strict=True, lower=True)
    heads = range(SB_HEADS)
    cols = [slice(h * SB_HEAD_DIM, (h + 1) * SB_HEAD_DIM) for h in heads]

    def key_block(s0, diagonal):
        zs = [_dot_nt(q_ref[:, cols[h]], k_ref[pl.ds(s0, CHUNK), cols[h]]) for h in heads]
        ls, lks, splits = [], [], []
        for h in heads:
            z = zs[h] * (SB_HEAD_DIM ** -0.5) + bias_ref[h]
            sp = _softplus_sum(z)
            lk = -sp
            if diagonal:
                lk = jnp.where(causal, lk, 0.0)
            ls.append(z - sp)
            lks.append(lk[:, 0:1])
            splits.append(jnp.concatenate(_split_bf16(lk), axis=1))
        betweens = [_dot(splits[h], upper2) for h in heads]
        ws = []
        for h in heads:
            between = betweens[h] if diagonal else betweens[h] + run_ref[h]
            w = jnp.exp(ls[h] + between)
            if diagonal:
                w = jnp.where(causal, w, 0.0)
            ws.append(w.astype(BF16))
            run_ref[h] = between[:, 0:1] + lks[h]
        outs = [_dot(ws[h], v_ref[pl.ds(s0, CHUNK), cols[h]]) for h in heads]
        for h in heads:
            if diagonal:
                acc_ref[:, cols[h]] = outs[h]
            else:
                acc_ref[:, cols[h]] += outs[h]

    key_block(pl.multiple_of(qi * CHUNK, CHUNK), True)

    def body(t, carry):
        key_block(pl.multiple_of((qi - t) * CHUNK, CHUNK), False)
        return carry

    lax.fori_loop(1, qi + 1, body, 0)
    o_ref[...] = acc_ref[...].astype(o_ref.dtype)


def _sb_prompt(q, k, v, bias, batch, seq):
    nq = seq // CHUNK
    return pl.pallas_call(
        _sb_prompt_kernel,
        grid=(batch, nq),
        in_specs=[pl.BlockSpec(memory_space=pltpu.SMEM),
                  pl.BlockSpec((CHUNK, SB_WIDTH), lambda b, i: (b * nq + i, 0)),
                  pl.BlockSpec((seq, SB_WIDTH), lambda b, i: (b, 0)),
                  pl.BlockSpec((seq, SB_WIDTH), lambda b, i: (b, 0))],
        out_specs=pl.BlockSpec((CHUNK, SB_WIDTH), lambda b, i: (b * nq + i, 0)),
        out_shape=jax.ShapeDtypeStruct((batch * seq, SB_WIDTH), BF16),
        scratch_shapes=[pltpu.VMEM((CHUNK, SB_WIDTH), F32), pltpu.VMEM((SB_HEADS, CHUNK, 1), F32)],
        compiler_params=_cparams(("arbitrary", "arbitrary")),
        name="sb_prompt",
    )(bias, q, k, v)


def _sb_sample_kernel(pt_ref, q_ref, bias_ref, *refs, n_pages):
    k_refs, v_refs, o_ref = refs[:n_pages], refs[n_pages:2 * n_pages], refs[2 * n_pages]
    row = lax.broadcasted_iota(jnp.int32, (SB_HEADS, SB_WIDTH), 0)
    col = lax.broadcasted_iota(jnp.int32, (SB_HEADS, SB_WIDTH), 1)
    own = (col // SB_HEAD_DIM) == row
    qm = jnp.where(own, jnp.broadcast_to(q_ref[...].astype(F32), (SB_HEADS, SB_WIDTH)), 0.0).astype(BF16)
    upper = _tri(CHUNK, strict=True, lower=True).astype(BF16)
    upper2 = jnp.concatenate([upper, upper], axis=0)
    bias = bias_ref[...]
    pages = range(n_pages)
    zs = [_dot_nt(qm, k_refs[p][...].astype(BF16)) * (SB_HEAD_DIM ** -0.5) + bias for p in pages]
    sps = [_softplus_sum(z) for z in zs]
    splits = [jnp.concatenate(_split_bf16(-sp), axis=1) for sp in sps]
    betweens = [_dot(s, upper2) for s in splits]
    run = jnp.zeros((SB_HEADS, 1), F32)
    acc = jnp.zeros((SB_HEADS, SB_WIDTH), F32)
    for p in reversed(pages):
        between = betweens[p] + run
        w = jnp.exp((zs[p] - sps[p]) + between)
        acc = acc + _dot(w.astype(BF16), v_refs[p][...].astype(BF16))
        run = between[:, 0:1] - sps[p][:, 0:1]
    o_ref[...] = jnp.sum(jnp.where(own, acc, 0.0), axis=0, keepdims=True).astype(o_ref.dtype)


def _sb_sample(q, ck, cv, layer, page_table, bias):
    bs, n_pages = page_table.shape
    page = ck.shape[2]
    page_spec = lambda p: pl.BlockSpec((None, None, page, SB_WIDTH),
                                       lambda b, pt, p=p: (layer, pt[b * n_pages + p], 0, 0))
    grid_spec = pltpu.PrefetchScalarGridSpec(
        num_scalar_prefetch=1,
        grid=(bs,),
        in_specs=[pl.BlockSpec((None, 1, SB_WIDTH), lambda b, pt: (b, 0, 0)),
                  pl.BlockSpec((SB_HEADS, 1), lambda b, pt: (0, 0))]
        + [page_spec(p) for p in range(n_pages)] * 2,
        out_specs=pl.BlockSpec((None, 1, SB_WIDTH), lambda b, pt: (b, 0, 0)),
    )
    out = pl.pallas_call(
        functools.partial(_sb_sample_kernel, n_pages=n_pages),
        grid_spec=grid_spec,
        out_shape=jax.ShapeDtypeStruct((bs, 1, SB_WIDTH), BF16),
        compiler_params=_cparams(("arbitrary",)),
        name="sb_sample",
    )(page_table.reshape(-1), q.reshape(bs, 1, SB_WIDTH), bias.reshape(SB_HEADS, 1),
      *([ck] * n_pages), *([cv] * n_pages))
    return out.reshape(bs, SB_WIDTH)


def _log_gamma(h):
    return math.log1p(-(2.0 ** (-5.0 - h)))


def _rope_tables(pos):
    half = RET_DIM // 2
    inv = ROPE_BASE ** (-jnp.arange(half, dtype=F32) / half)
    ang = pos.astype(F32)[:, None] * inv[None, :]
    cos, sin = jnp.cos(ang), jnp.sin(ang)
    return jnp.concatenate([cos, cos], axis=-1), jnp.concatenate([-sin, sin], axis=-1)


def _rope(x, cos2, sin2):
    return x * cos2 + pltpu.roll(x, RET_DIM // 2, 1) * sin2


def _group_norm_gate(o, gn, g):
    mu = jnp.mean(o, axis=-1, keepdims=True)
    c = o - mu
    var = jnp.mean(c * c, axis=-1, keepdims=True)
    return _silu(g) * ((c * lax.rsqrt(var + EPS)) * gn)


def _ret_prompt_kernel(q_ref, k_ref, v_ref, g_ref, cos_ref, sin_ref, gn_ref, o_ref, r_ref, state):
    c = pl.program_id(1)

    @pl.when(c == 0)
    def _():
        state[...] = jnp.zeros_like(state)

    cos2, sin2 = cos_ref[...], sin_ref[...]
    li = lax.broadcasted_iota(jnp.int32, (CHUNK, CHUNK), 0)
    mi = lax.broadcasted_iota(jnp.int32, (CHUNK, CHUNK), 1)
    diff = li - mi
    idx = lax.broadcasted_iota(jnp.int32, (CHUNK, 1), 0).astype(F32)
    heads = range(RET_HEADS)
    lgs = [_log_gamma(h) for h in heads]
    cols = [slice(h * RET_DIM, (h + 1) * RET_DIM) for h in heads]
    qbs = [_rope(q_ref[:, cols[h]], cos2, sin2).astype(BF16) for h in heads]
    ks = [_rope(k_ref[:, cols[h]], cos2, sin2) * (RET_DIM ** -0.5) for h in heads]
    vs = [v_ref[:, cols[h]].astype(BF16) for h in heads]
    r_olds = [state[h] for h in heads]
    raw = [_dot_nt(qbs[h], ks[h].astype(BF16)) for h in heads]
    carried = [_dot(qbs[h], r_olds[h].astype(BF16)) for h in heads]
    grown = [_dot_tn((ks[h] * jnp.exp((CHUNK - 1.0 - idx) * lgs[h])).astype(BF16), vs[h]) for h in heads]
    scores = [raw[h] * jnp.where(diff >= 0, jnp.exp(jnp.maximum(diff, 0).astype(F32) * lgs[h]), 0.0) for h in heads]
    inner = [_dot(scores[h].astype(BF16), vs[h]) for h in heads]
    for h in heads:
        state[h] = math.exp(CHUNK * lgs[h]) * r_olds[h] + grown[h]
        from_state = carried[h] * jnp.exp((idx + 1.0) * lgs[h])
        o = _group_norm_gate(inner[h] + from_state, gn_ref[:, cols[h]], g_ref[:, cols[h]])
        o_ref[:, cols[h]] = o.astype(o_ref.dtype)

    @pl.when(c == pl.num_programs(1) - 1)
    def _():
        r_ref[...] = state[...]


def _ret_prompt(ret, gn_w, batch, seq):
    nc = seq // CHUNK
    cos2, sin2 = _rope_tables(jnp.arange(seq, dtype=jnp.int32))
    col = lambda j: pl.BlockSpec((CHUNK, RET_WIDTH), lambda b, c, j=j: (b * nc + c, j))
    tab = pl.BlockSpec((CHUNK, RET_DIM), lambda b, c: (c, 0))
    return pl.pallas_call(
        _ret_prompt_kernel,
        grid=(batch, nc),
        in_specs=[col(0), col(1), col(2), col(3), tab, tab, pl.BlockSpec((1, RET_WIDTH), lambda b, c: (0, 0))],
        out_specs=[pl.BlockSpec((CHUNK, RET_WIDTH), lambda b, c: (b * nc + c, 0)),
                   pl.BlockSpec((None, RET_HEADS, RET_DIM, RET_DIM), lambda b, c: (b, 0, 0, 0))],
        out_shape=[jax.ShapeDtypeStruct((batch * seq, RET_WIDTH), BF16),
                   jax.ShapeDtypeStruct((batch, RET_HEADS, RET_DIM, RET_DIM), F32)],
        scratch_shapes=[pltpu.VMEM((RET_HEADS, RET_DIM, RET_DIM), F32)],
        compiler_params=_cparams(("arbitrary", "arbitrary")),
        name="ret_prompt",
    )(ret, ret, ret, ret, cos2, sin2, gn_w)


RET_SEQS = 8


def _ret_sample_kernel(q_ref, k_ref, v_ref, g_ref, cos_ref, sin_ref, gn_ref, r0_ref, o_ref, r_ref):
    cos2, sin2 = cos_ref[...], sin_ref[...]
    for h in range(RET_HEADS):
        gamma = math.exp(_log_gamma(h))
        cols = slice(h * RET_DIM, (h + 1) * RET_DIM)
        q = _rope(q_ref[:, cols], cos2, sin2)
        k = _rope(k_ref[:, cols], cos2, sin2) * (RET_DIM ** -0.5)
        v = v_ref[:, cols]
        qb, kb, vb = q.astype(BF16), k.astype(BF16), v.astype(BF16)
        qk = jnp.sum(qb.astype(F32) * kb.astype(F32), axis=-1, keepdims=True)
        inner = qk * vb.astype(F32)
        rows = []
        for s in range(RET_SEQS):
            r_old = r0_ref[s, h]
            rows.append(_dot(qb[s:s + 1, :], r_old.astype(BF16)) * gamma)
            k_col = jnp.transpose(kb[s:s + 1, :].astype(F32))
            r_ref[s, h] = gamma * r_old + k_col * vb[s:s + 1, :].astype(F32)
        o = inner + jnp.concatenate(rows, axis=0)
        o_ref[:, cols] = _group_norm_gate(o, gn_ref[:, cols], g_ref[:, cols]).astype(o_ref.dtype)


def _ret_sample(ret, gn_w, r0, layer, past_len):
    bs = ret.shape[0]
    cos2, sin2 = _rope_tables(jnp.full((1,), past_len, jnp.int32))
    col = lambda j: pl.BlockSpec((RET_SEQS, RET_WIDTH), lambda i, j=j: (i, j))
    one = lambda n: pl.BlockSpec((1, n), lambda i: (0, 0))
    st_in = pl.BlockSpec((None, RET_SEQS, RET_HEADS, RET_DIM, RET_DIM), lambda i: (layer, i, 0, 0, 0))
    st = pl.BlockSpec((RET_SEQS, RET_HEADS, RET_DIM, RET_DIM), lambda i: (i, 0, 0, 0))
    return pl.pallas_call(
        _ret_sample_kernel,
        grid=(bs // RET_SEQS,),
        in_specs=[col(0), col(1), col(2), col(3), one(RET_DIM), one(RET_DIM), one(RET_WIDTH), st_in],
        out_specs=[pl.BlockSpec((RET_SEQS, RET_WIDTH), lambda i: (i, 0)), st],
        out_shape=[jax.ShapeDtypeStruct((bs, RET_WIDTH), BF16),
                   jax.ShapeDtypeStruct(r0.shape[1:], F32)],
        compiler_params=_cparams(("arbitrary",)),
        name="ret_sample",
    )(ret, ret, ret, ret, cos2, sin2, gn_w, r0)


def _gated_group_rms(y, z, nw, groups):
    yg = y * _silu(z)
    gw = y.shape[-1] // groups
    outs = []
    for g in range(groups):
        a = yg[:, g * gw:(g + 1) * gw]
        outs.append(a * lax.rsqrt(jnp.mean(a * a, axis=-1, keepdims=True) + EPS))
    return jnp.concatenate(outs, axis=-1) * nw


def _ssd_prompt_kernel(z_ref, xbc_ref, dt_ref, cw_ref, cb_ref, dtb_ref, alog_ref, dsk_ref, nw_ref, sel_ref,
                       y_ref, s_ref, cv_ref, state, win, ybuf, *, heads, width):
    c = pl.program_id(1)
    hpg = heads // SSD_GROUPS
    gn = SSD_GROUPS * SSD_STATE

    @pl.when(c == 0)
    def _():
        state[...] = jnp.zeros_like(state)
        win[0:8, :] = jnp.zeros((8, win.shape[1]), F32)

    win[8:8 + CHUNK, :] = xbc_ref[...]
    conv = cb_ref[...]
    for j in range(SSD_CONV):
        conv = conv + win[5 + j:5 + j + CHUNK, :] * cw_ref[j:j + 1, :]
    act = _silu(conv)

    @pl.when(c == pl.num_programs(1) - 1)
    def _():
        cv_ref[...] = win[CHUNK + 5:CHUNK + 8, :]

    win[0:8, :] = win[CHUNK:CHUNK + 8, :]

    dt = _softplus(dt_ref[...] + dtb_ref[...])
    a = -jnp.exp(alog_ref[...])
    da_hi, da_lo = _split_bf16(dt * a)
    lower = _tri(CHUNK, strict=False, lower=True).astype(BF16)
    acs = _dot(lower, da_hi) + _dot(lower, da_lo)
    acs_t = jnp.transpose(acs)
    dt_t = jnp.transpose(dt)
    both = jnp.concatenate([acs, dt], axis=0)
    p0 = both.astype(BF16)
    r0 = both - p0.astype(F32)
    p1 = r0.astype(BF16)
    p2 = (r0 - p1.astype(F32)).astype(BF16)
    wide = _dot(jnp.concatenate([p0, p1, p2], axis=1), sel_ref[...])
    causal = _tri(CHUNK, strict=False, lower=True)
    p = SSD_HEAD_DIM
    for g in range(SSD_GROUPS):
        bg = act[:, width + g * SSD_STATE:width + (g + 1) * SSD_STATE].astype(BF16)
        cg = act[:, width + gn + g * SSD_STATE:width + gn + (g + 1) * SSD_STATE].astype(BF16)
        cbm = _dot_nt(cg, bg)
        for r in range(hpg):
            h = g * hpg + r
            cols = slice(h * p, (h + 1) * p)
            xh = act[:, cols]
            acs_b = wide[:CHUNK, h * LANES:(h + 1) * LANES]
            dt_b = wide[CHUNK:, h * LANES:h * LANES + p]
            seg = jnp.where(causal, acs_b - acs_t[h:h + 1, :], -jnp.inf)
            w = cbm * jnp.exp(seg) * dt_t[h:h + 1, :]
            s_old = state[h]
            y = _dot(w.astype(BF16), xh.astype(BF16))
            y = y + _dot_nt(cg, s_old.astype(BF16)) * jnp.exp(acs_b[:, :p])
            last = acs_b[CHUNK - 1:CHUNK, :]
            to_end = jnp.exp(last[:, :p] - acs_b[:, :p]) * dt_b
            state[h] = jnp.exp(last) * s_old + _dot_tn((xh * to_end).astype(BF16), bg)
            ybuf[:, cols] = y + dsk_ref[:, cols] * xh

    y_ref[...] = _gated_group_rms(ybuf[...], z_ref[...], nw_ref[...], SSD_GROUPS).astype(y_ref.dtype)

    @pl.when(c == pl.num_programs(1) - 1)
    def _():
        s_ref[...] = state[...]


def _ssd_prompt(z, xbc, dt, cw, cb, dtb, alog, dsk, nw, batch, seq, heads):
    nc = seq // CHUNK
    width = z.shape[1]
    cdim = xbc.shape[1]
    row = lambda n: pl.BlockSpec((CHUNK, n), lambda b, c: (b * nc + c, 0))
    full = lambda a: pl.BlockSpec(a.shape, lambda b, c: (0, 0))
    lane = jnp.arange(LANES, dtype=jnp.int32)[:, None]
    tile = jnp.arange(heads * LANES, dtype=jnp.int32)[None, :] // LANES
    sel = jnp.tile((lane == tile).astype(BF16), (3, 1))
    return pl.pallas_call(
        functools.partial(_ssd_prompt_kernel, heads=heads, width=width),
        grid=(batch, nc),
        in_specs=[row(width), row(cdim), row(LANES), full(cw), full(cb), full(dtb), full(alog), full(dsk), full(nw),
                  full(sel)],
        out_specs=[row(width),
                   pl.BlockSpec((None, heads, SSD_HEAD_DIM, SSD_STATE), lambda b, c: (b, 0, 0, 0)),
                   pl.BlockSpec((None, SSD_CONV - 1, cdim), lambda b, c: (b, 0, 0))],
        out_shape=[jax.ShapeDtypeStruct((batch * seq, width), BF16),
                   jax.ShapeDtypeStruct((batch, heads, SSD_HEAD_DIM, SSD_STATE), F32),
                   jax.ShapeDtypeStruct((batch, SSD_CONV - 1, cdim), F32)],
        scratch_shapes=[pltpu.VMEM((heads, SSD_HEAD_DIM, SSD_STATE), F32),
                        pltpu.VMEM((CHUNK + 8, cdim), F32),
                        pltpu.VMEM((CHUNK, width), F32)],
        compiler_params=_cparams(("arbitrary", "arbitrary")),
        name="ssd_prompt",
    )(z, xbc, dt, cw, cb, dtb, alog, dsk, nw, sel)


SSD_SEQS = 4


def _ssd_sample_kernel(z_ref, xbc_ref, dt_ref, cprev_ref, cw_ref, cb_ref, dtb_ref, alog_ref, dsk_ref, nw_ref,
                       s0_ref, y_ref, s_ref, cv_ref, *, heads, width):
    hpg = heads // SSD_GROUPS
    gn = SSD_GROUPS * SSD_STATE
    for s in range(SSD_SEQS):
        dt = _softplus(dt_ref[s] + dtb_ref[...])
        decay = jnp.exp(dt * (-jnp.exp(alog_ref[...])))
        xrow = xbc_ref[s]
        prev = cprev_ref[s]
        conv = cb_ref[...] + xrow * cw_ref[SSD_CONV - 1:SSD_CONV, :]
        for j in range(SSD_CONV - 1):
            conv = conv + prev[j:j + 1, :] * cw_ref[j:j + 1, :]
        cv_ref[s, 0:SSD_CONV - 2, :] = prev[1:SSD_CONV - 1, :]
        cv_ref[s, SSD_CONV - 2:SSD_CONV - 1, :] = xrow
        act = _silu(conv)
        yrow = []
        for g in range(SSD_GROUPS):
            bg = act[:, width + g * SSD_STATE:width + (g + 1) * SSD_STATE].astype(BF16).astype(F32)
            cg = act[:, width + gn + g * SSD_STATE:width + gn + (g + 1) * SSD_STATE].astype(BF16)
            for r in range(hpg):
                h = g * hpg + r
                cols = slice(h * SSD_HEAD_DIM, (h + 1) * SSD_HEAD_DIM)
                xh = act[:, cols]
                dtx = (xh * dt[:, h:h + 1]).astype(BF16).astype(F32)
                s_new = decay[:, h:h + 1] * s0_ref[s, h] + jnp.transpose(dtx) * bg
                s_ref[s, h] = s_new
                yrow.append(_dot_nt(cg, s_new.astype(BF16)) + dsk_ref[:, cols] * xh)
        y = jnp.concatenate(yrow, axis=-1)
        y_ref[s] = _gated_group_rms(y, z_ref[s], nw_ref[...], SSD_GROUPS).astype(y_ref.dtype)


def _ssd_sample(z, xbc, dt, conv_prev, s0, layer, cw, cb, dtb, alog, dsk, nw, heads):
    bs, width = z.shape
    cdim = xbc.shape[1]
    row = lambda n: pl.BlockSpec((SSD_SEQS, 1, n), lambda i: (i, 0, 0))
    full = lambda a: pl.BlockSpec(a.shape, lambda i: (0, 0))
    st_in = pl.BlockSpec((None, SSD_SEQS, heads, SSD_HEAD_DIM, SSD_STATE), lambda i: (layer, i, 0, 0, 0))
    cv_in = pl.BlockSpec((None, SSD_SEQS, SSD_CONV - 1, cdim), lambda i: (layer, i, 0, 0))
    st = pl.BlockSpec((SSD_SEQS, heads, SSD_HEAD_DIM, SSD_STATE), lambda i: (i, 0, 0, 0))
    cv = pl.BlockSpec((SSD_SEQS, SSD_CONV - 1, cdim), lambda i: (i, 0, 0))
    y, s_new, cv_new = pl.pallas_call(
        functools.partial(_ssd_sample_kernel, heads=heads, width=width),
        grid=(bs // SSD_SEQS,),
        in_specs=[row(width), row(cdim), row(LANES), cv_in, full(cw), full(cb), full(dtb), full(alog), full(dsk),
                  full(nw), st_in],
        out_specs=[row(width), st, cv],
        out_shape=[jax.ShapeDtypeStruct((bs, 1, width), BF16),
                   jax.ShapeDtypeStruct(s0.shape[1:], F32),
                   jax.ShapeDtypeStruct(conv_prev.shape[1:], F32)],
        compiler_params=_cparams(("arbitrary",)),
        name="ssd_sample",
    )(z.reshape(bs, 1, width), xbc.reshape(bs, 1, cdim), dt.reshape(bs, 1, LANES), conv_prev, cw, cb, dtb, alog,
      dsk, nw, s0)
    return y.reshape(bs, width), s_new, cv_new


def _expert_kernel(be_ref, nu_ref, x_ref, w1_ref, b1_ref, w2_ref, b2_ref, o_ref, w1s, w2s, *, f):
    i = pl.program_id(0)
    prev = be_ref[jnp.maximum(i - 1, 0)]

    @pl.when((i == 0) | (be_ref[i] != prev))
    def _():
        w1s[...] = w1_ref[...].astype(BF16)
        w2s[...] = w2_ref[...].astype(BF16)

    @pl.when(i < nu_ref[0])
    def _():
        x = x_ref[...].astype(BF16)
        acc = jnp.zeros(o_ref.shape, F32) + b2_ref[...]
        half = f // 2
        for c in range(2):
            glu = _dot(x, w1s[:, c * half:(c + 1) * half]) + b1_ref[:, c * half:(c + 1) * half]
            lin = _dot(x, w1s[:, f + c * half:f + (c + 1) * half]) + b1_ref[:, f + c * half:f + (c + 1) * half]
            glu = jnp.minimum(glu, SWIGLU_LIMIT)
            lin = jnp.clip(lin, -SWIGLU_LIMIT, SWIGLU_LIMIT)
            a = glu * jax.nn.sigmoid(SWIGLU_ALPHA * glu) * (lin + 1.0)
            acc = acc + _dot(a.astype(BF16), w2s[c * half:(c + 1) * half, :])
        o_ref[...] = acc

    @pl.when(i >= nu_ref[0])
    def _():
        o_ref[...] = jnp.zeros_like(o_ref)


def _experts(block_expert, n_used, xb, layer, w1, b1, w2, b2):
    n_rows = xb.shape[0]
    _, n_exp, d, f2 = w1.shape
    f = f2 // 2
    n_blocks = n_rows // MOE_ROWS
    rows = pl.BlockSpec((MOE_ROWS, d), lambda i, be, nu: (i, 0))
    grid_spec = pltpu.PrefetchScalarGridSpec(
        num_scalar_prefetch=2,
        grid=(n_blocks,),
        in_specs=[rows,
                  pl.BlockSpec((None, None, d, f2), lambda i, be, nu: (layer, be[i], 0, 0)),
                  pl.BlockSpec((None, None, 1, f2), lambda i, be, nu: (layer, be[i], 0, 0)),
                  pl.BlockSpec((None, None, f, d), lambda i, be, nu: (layer, be[i], 0, 0)),
                  pl.BlockSpec((None, None, 1, d), lambda i, be, nu: (layer, be[i], 0, 0))],
        out_specs=rows,
        scratch_shapes=[pltpu.VMEM((d, f2), BF16), pltpu.VMEM((f, d), BF16)],
    )
    depth = w1.shape[0]
    return pl.pallas_call(
        functools.partial(_expert_kernel, f=f),
        grid_spec=grid_spec,
        out_shape=jax.ShapeDtypeStruct(xb.shape, F32),
        compiler_params=_cparams(("arbitrary",)),
        name="moe_experts",
    )(block_expert, n_used, xb, w1, b1.reshape(depth, n_exp, 1, f2), w2, b2.reshape(depth, n_exp, 1, d))


def _expert_layout(counts, n_blocks):
    n_exp = counts.shape[0]
    padded = (counts + MOE_ROWS - 1) // MOE_ROWS * MOE_ROWS
    pad_end = jnp.cumsum(padded)
    starts = jnp.arange(n_blocks, dtype=jnp.int32) * MOE_ROWS
    block_expert = jnp.minimum(jnp.sum(starts[:, None] >= pad_end[None, :], axis=1), n_exp - 1).astype(jnp.int32)
    return pad_end - padded, block_expert, (pad_end[-1] // MOE_ROWS).astype(jnp.int32).reshape(1)


def _row_positions(rec, pad_start):
    n_exp = pad_start.shape[0]
    idx = rec[:, ROUTE_IDX:ROUTE_IDX + TOP_K].astype(jnp.int32)
    rank = rec[:, ROUTE_RANK:ROUTE_RANK + TOP_K].astype(jnp.int32)
    start = jnp.sum(jnp.where(idx[..., None] == jnp.arange(n_exp, dtype=jnp.int32), pad_start, 0), axis=-1)
    return (start + rank).reshape(-1).astype(jnp.int32)


SUBLANES = 8
COMBINE_ROWS = 32


def _dispatch_kernel(pos_ref, h_ref, xb_in, xb_ref, sem):
    del xb_in
    tm = h_ref.shape[0]
    base = pl.program_id(0) * (tm * TOP_K)

    def row_copy(g, k, j):
        t0 = pl.multiple_of(g * SUBLANES, SUBLANES)
        dst = pos_ref[base + g * (SUBLANES * TOP_K) + (k * TOP_K + j)]
        return pltpu.make_async_copy(h_ref.at[pl.ds(t0 + k, 1), :], xb_ref.at[pl.ds(dst, 1), :], sem)

    def issue(g, carry):
        for k in range(SUBLANES):
            for j in range(TOP_K):
                row_copy(g, k, j).start(priority=j % 2)
        return carry

    def drain(g, carry):
        for k in range(SUBLANES):
            for j in range(TOP_K):
                row_copy(g, k, j).wait()
        return carry

    lax.fori_loop(0, tm // SUBLANES, issue, 0)
    lax.fori_loop(0, tm // SUBLANES, drain, 0)


def _dispatch(pos, h, xb, name):
    t, d = h.shape
    tm = min(ROW_TILE, t)
    grid_spec = pltpu.PrefetchScalarGridSpec(
        num_scalar_prefetch=1,
        grid=(t // tm,),
        in_specs=[pl.BlockSpec((tm, d), lambda i, p: (i, 0)),
                  pl.BlockSpec(memory_space=pl.ANY)],
        out_specs=pl.BlockSpec(memory_space=pl.ANY),
        scratch_shapes=[pltpu.SemaphoreType.DMA],
    )
    return pl.pallas_call(
        _dispatch_kernel,
        grid_spec=grid_spec,
        out_shape=jax.ShapeDtypeStruct(xb.shape, xb.dtype),
        input_output_aliases={2: 0},
        compiler_params=_cparams(("arbitrary",)),
        name=name,
    )(pos, h, xb)


def _combine_kernel(pos_ref, x_ref, g_ref, rec_ref, yb_ref, *rest, final):
    if final:
        fw_ref, o_ref, ybuf, sem = rest
    else:
        o_ref, ybuf, sem = rest
    tm = x_ref.shape[0]
    base = pl.program_id(0) * (tm * TOP_K)

    def row_copy(g, k, j):
        t0 = pl.multiple_of(g * SUBLANES, SUBLANES)
        src = pos_ref[base + g * (SUBLANES * TOP_K) + (k * TOP_K + j)]
        return pltpu.make_async_copy(yb_ref.at[pl.ds(src, 1), :], ybuf.at[j, pl.ds(t0 + k, 1), :], sem)

    def issue(g, carry):
        for k in range(SUBLANES):
            for j in range(TOP_K):
                row_copy(g, k, j).start(priority=j % 2)
        return carry

    def drain(g, carry):
        for k in range(SUBLANES):
            for j in range(TOP_K):
                row_copy(g, k, j).wait()
        return carry

    lax.fori_loop(0, tm // SUBLANES, issue, 0)
    lax.fori_loop(0, tm // SUBLANES, drain, 0)

    def rows(c, carry):
        r = pl.ds(pl.multiple_of(c * COMBINE_ROWS, COMBINE_ROWS), COMBINE_ROWS)
        rec = rec_ref[r, :]
        y = rec[:, ROUTE_GATE:ROUTE_GATE + 1] * ybuf[0, r, :]
        for j in range(1, TOP_K):
            y = y + rec[:, ROUTE_GATE + j:ROUTE_GATE + j + 1] * ybuf[j, r, :]
        g = g_ref[...] if g_ref.shape[0] == 1 else g_ref[r, :]
        xn = x_ref[r, :] + g * y
        if final:
            xn = (xn * lax.rsqrt(jnp.mean(xn * xn, axis=-1, keepdims=True) + EPS)) * fw_ref[...]
        o_ref[r, :] = xn
        return carry

    lax.fori_loop(0, tm // COMBINE_ROWS, rows, 0)


def _combine(pos, x, mod, kind, rec, yb, tiles_per_seq, final_w, name):
    t, d = x.shape
    tm = min(ROW_TILE, t)
    (g_spec,) = _mod_specs(kind, (5,), tm, tiles_per_seq, d)
    with_p = lambda spec: pl.BlockSpec(spec.block_shape, lambda i, p, f=spec.index_map: f(i))
    in_specs = [pl.BlockSpec((tm, d), lambda i, p: (i, 0)), with_p(g_spec),
                pl.BlockSpec((tm, LANES), lambda i, p: (i, 0)),
                pl.BlockSpec(memory_space=pl.ANY)]
    args = [pos, x, mod, rec, yb]
    if final_w is not None:
        in_specs.append(pl.BlockSpec((1, d), lambda i, p: (0, 0)))
        args.append(final_w)
    grid_spec = pltpu.PrefetchScalarGridSpec(
        num_scalar_prefetch=1,
        grid=(t // tm,),
        in_specs=in_specs,
        out_specs=pl.BlockSpec((tm, d), lambda i, p: (i, 0)),
        scratch_shapes=[pltpu.VMEM((TOP_K, tm, d), F32), pltpu.SemaphoreType.DMA],
    )
    return pl.pallas_call(
        functools.partial(_combine_kernel, final=final_w is not None),
        grid_spec=grid_spec,
        out_shape=jax.ShapeDtypeStruct((t, d), F32),
        compiler_params=_cparams(("arbitrary",)),
        name=name,
    )(*args)


def kernel(x_prompt, x_sample, cache_sb_k, cache_sb_v, page_table, state_ret, state_ssd, state_conv,
           c_prompt, c_sample, norm1_w, norm2_w, w_ada, b_ada, ab_w_in, ab_w_out, ret_gn_w, sb_bias,
           ssd_w_in, ssd_conv_w, ssd_conv_b, ssd_dt_bias, ssd_a_log, ssd_d, ssd_norm_w, ssd_w_out,
           router_w, router_b, moe_w1, moe_b1, moe_w2, moe_b2, final_norm_w):
    bp, sp, d = x_prompt.shape
    bs, ls, _ = x_sample.shape
    assert ls == 1 and sp % CHUNK == 0 and cache_sb_k.shape[2] == CHUNK
    depth = w_ada.shape[0]
    n_exp = router_w.shape[-1]
    tp = bp * sp
    past_len = page_table.shape[1] * cache_sb_k.shape[2]
    ssd_width = ssd_norm_w.shape[-1]
    ssd_heads = ssd_a_log.shape[-1]
    cdim = ssd_conv_w.shape[-1]
    tps = sp // min(ROW_TILE, tp)
    assert d % LANES == 0 and n_exp <= LANES
    n_blocks = -(-((tp + bs) * TOP_K) // MOE_ROWS) + n_exp

    mod = _ada_mod(jnp.concatenate([c_prompt, c_sample], axis=0), w_ada, b_ada)
    xp = x_prompt.reshape(tp, d)
    xs = x_sample.reshape(bs, d)
    row1 = lambda a: a.reshape(1, -1)
    pad_l = lambda a: jnp.pad(a, ((0, 0), (0, LANES - a.shape[-1])))

    sbk_p, sbv_p, sbk_s, sbv_s, ret_p, ret_s = [], [], [], [], [], []
    ssd_p, ssd_s, conv_p, conv_s = [], [], [], []
    y_prompt = y_sample = None
    xb = jnp.zeros((n_blocks * MOE_ROWS, d), F32)
    pool = cache_sb_k.shape[:3] + (SB_WIDTH,)
    ck = cache_sb_k.reshape(pool)
    cv = cache_sb_v.reshape(pool)
    for layer in range(depth):
        i = layer // 2
        modp = mod[layer, :bp].reshape(bp * 6, 1, d)
        mods = mod[layer, bp:]
        nw1 = row1(norm1_w[layer])
        if layer % 2 == 0:
            w_in = ab_w_in[i].astype(BF16)
            ws = [w_in[:, :SB_WIDTH], w_in[:, SB_WIDTH:2 * SB_WIDTH], w_in[:, 2 * SB_WIDTH:3 * SB_WIDTH],
                  w_in[:, 3 * SB_WIDTH:]]
            plan = ((BF16,), (F32, BF16), (F32, BF16), (F32,))
            qa_p, ka_p, kab_p, va_p, vab_p, rt_p = _inproj(xp, nw1, modp, "p", ws, plan, tps, "ab_in_prompt")
            qa_s, ka_s, _, va_s, _, rt_s = _inproj(xs, nw1, mods, "s", ws, plan, 1, "ab_in_sample")
            gn = row1(ret_gn_w[i])
            o_sb_p = _sb_prompt(qa_p, kab_p, vab_p, sb_bias[i], bp, sp)
            o_rt_p, r_p = _ret_prompt(rt_p, gn, bp, sp)
            o_sb_s = _sb_sample(qa_s, ck, cv, i, page_table, sb_bias[i])
            o_rt_s, r_s = _ret_sample(rt_s, gn, state_ret, i, past_len)
            parts_p, parts_s = [o_sb_p, o_rt_p], [o_sb_s, o_rt_s]
            w_out = ab_w_out[i].astype(BF16)
            w_parts = [w_out[:SB_WIDTH], w_out[SB_WIDTH:]]
            sbk_p.append(ka_p.reshape(bp, sp, SB_HEADS, SB_HEAD_DIM))
            sbv_p.append(va_p.reshape(bp, sp, SB_HEADS, SB_HEAD_DIM))
            sbk_s.append(ka_s.reshape(bs, 1, SB_HEADS, SB_HEAD_DIM))
            sbv_s.append(va_s.reshape(bs, 1, SB_HEADS, SB_HEAD_DIM))
            ret_p.append(r_p)
            ret_s.append(r_s)
        else:
            w_in = ssd_w_in[i]
            ws = [w_in[:, :ssd_width].astype(BF16), w_in[:, ssd_width:ssd_width + cdim].astype(BF16),
                  pad_l(w_in[:, ssd_width + cdim:]).astype(BF16)]
            plan = ((F32,), (F32,), (F32,))
            z_p, xbc_p, dt_p = _inproj(xp, nw1, modp, "p", ws, plan, tps, "ssd_in_prompt")
            z_s, xbc_s, dt_s = _inproj(xs, nw1, mods, "s", ws, plan, 1, "ssd_in_sample")
            consts = (ssd_conv_w[i], row1(ssd_conv_b[i]), pad_l(row1(ssd_dt_bias[i])), pad_l(row1(ssd_a_log[i])),
                      row1(jnp.repeat(ssd_d[i], SSD_HEAD_DIM)), row1(ssd_norm_w[i]))
            mixed_p, s_p, cv_p = _ssd_prompt(z_p, xbc_p, dt_p, *consts, bp, sp, ssd_heads)
            mixed_s, s_s, cv_s = _ssd_sample(z_s, xbc_s, dt_s, state_conv, state_ssd, i, *consts, ssd_heads)
            parts_p, parts_s = [mixed_p], [mixed_s]
            w_parts = [ssd_w_out[i].astype(BF16)]
            ssd_p.append(s_p)
            ssd_s.append(s_s)
            conv_p.append(cv_p)
            conv_s.append(cv_s)

        rw = pad_l(router_w[layer]).astype(BF16)
        rb = jnp.pad(row1(router_b[layer]), ((0, 0), (0, LANES - n_exp)), constant_values=ROUTE_PAD_LOGIT)
        nw2 = row1(norm2_w[layer])
        cnt0 = jnp.zeros((1, LANES), F32)
        xp, h2p, rec_p, cnt_p = _post(parts_p, w_parts, xp, nw2, modp, "p", rw, rb, cnt0, tps, "post_prompt")
        xs, h2s, rec_s, cnt = _post(parts_s, w_parts, xs, nw2, mods, "s", rw, rb, cnt_p, 1, "post_sample")

        pad_start, block_expert, n_used = _expert_layout(cnt[0, :n_exp].astype(jnp.int32), n_blocks)
        pos_p = _row_positions(rec_p, pad_start)
        pos_s = _row_positions(rec_s, pad_start)
        xb = _dispatch(pos_p, h2p, xb, "dispatch_prompt")
        xb = _dispatch(pos_s, h2s, xb, "dispatch_sample")
        yb = _experts(block_expert, n_used, xb, layer, moe_w1, moe_b1, moe_w2, moe_b2)
        last = layer == depth - 1
        fw = row1(final_norm_w) if last else None
        xp = _combine(pos_p, xp, modp, "p", rec_p, yb, tps, fw, "combine_prompt")
        xs = _combine(pos_s, xs, mods, "s", rec_s, yb, 1, fw, "combine_sample")

    st = lambda lst: jnp.stack(lst, axis=0)
    return (xp.reshape(bp, sp, d), xs.reshape(bs, ls, d), st(sbk_p), st(sbv_p), st(sbk_s), st(sbv_s),
            st(ret_p), st(ret_s), st(ssd_p), st(ssd_s), st(conv_p), st(conv_s))
```
